```python
import math
import jax, jax.numpy as jnp
from jax import lax
import numpy as np

D_MODEL = 1024
BATCH = 2
SEQ = 16384
DEPTH = 2

CHUNK = 64
N_MEM = 256
MLSTM_HEADS = 4
MLSTM_WIDTH = D_MODEL
MLSTM_HEAD_DIM = MLSTM_WIDTH // MLSTM_HEADS
QKV_BLOCK = 4
MLSTM_CONV = 4
CONV_WIDTH = D_MODEL
CONV_KERNEL = 31
XATTN_HEADS = 4
XATTN_HEAD_DIM = D_MODEL // XATTN_HEADS
D_FF = 2816
FFN_CONV = 3
RMS_EPS = 1e-6
LN_EPS = 1e-5
IN_COLS = 2 * MLSTM_WIDTH + 2 * CONV_WIDTH + 2 * D_MODEL

kernel_name = "hybrid_mlstm_conformer_conv_xattn_convffn"


def rmsnorm(x, g):
    xf = x.astype(jnp.float32)
    y = xf * lax.rsqrt(jnp.mean(xf * xf, -1, keepdims=True) + RMS_EPS)
    return (y * g.astype(jnp.float32)).astype(x.dtype)


def layernorm(x, g, b):
    xf = x.astype(jnp.float32)
    mu = jnp.mean(xf, -1, keepdims=True)
    var = jnp.mean(jnp.square(xf - mu), -1, keepdims=True)
    y = (xf - mu) * lax.rsqrt(var + LN_EPS)
    return (y * g.astype(jnp.float32) + b.astype(jnp.float32)).astype(x.dtype)


def causal_dwconv(x, w, b):
    K, C = w.shape
    y = lax.conv_general_dilated(
        x, w[:, None, :].astype(x.dtype), window_strides=(1,), padding=[(K - 1, 0)],
        dimension_numbers=("NWC", "WIO", "NWC"), feature_group_count=C)
    return y + b.astype(x.dtype)


def mlstm_chunkwise(q, k, v, i_pre, f_pre):
    B, H, S, dh = q.shape
    nc = S // CHUNK

    def to_chunks(t):
        t = t.reshape((B, H, nc, CHUNK) + t.shape[3:])
        return jnp.moveaxis(t, 2, 0)

    qc = to_chunks(q)
    kc = to_chunks(k * (dh ** -0.5))
    vc = to_chunks(v)
    ic = to_chunks(i_pre)
    lfc = to_chunks(jax.nn.log_sigmoid(f_pre))
    causal = jnp.tril(jnp.ones((CHUNK, CHUNK), dtype=bool))

    def step(carry, inp):
        C, n, m = carry
        q_, k_, v_, i_, lf_ = inp
        b = jnp.cumsum(lf_, axis=-1)
        a = b + m[..., None]
        dmat = b[..., :, None] - b[..., None, :] + i_[..., None, :]
        dmat = jnp.where(causal, dmat, -jnp.inf)
        m_row = jnp.maximum(a, jnp.max(dmat, axis=-1))
        w_inter = jnp.exp(a - m_row)
        s = jnp.einsum("bhld,bhsd->bhls", q_, k_) * jnp.exp(dmat - m_row[..., None])
        num = (jnp.einsum("bhls,bhsd->bhld", s, v_)
               + w_inter[..., None] * jnp.einsum("bhld,bhde->bhle", q_, C))
        den = jnp.sum(s, axis=-1) + w_inter * jnp.einsum("bhld,bhd->bhl", q_, n)
        h = num / jnp.maximum(jnp.abs(den), jnp.exp(-m_row))[..., None]
        b_end = b[..., -1]
        g = b_end[..., None] - b + i_
        m_new = jnp.maximum(b_end + m, jnp.max(g, axis=-1))
        decay = jnp.exp(b_end + m - m_new)
        wk = jnp.exp(g - m_new[..., None])[..., None] * k_
        C_new = decay[..., None, None] * C + jnp.einsum("bhld,bhle->bhde", wk, v_)
        n_new = decay[..., None] * n + jnp.sum(wk, axis=2)
        return (C_new, n_new, m_new), h

    init = (jnp.zeros((B, H, dh, dh), jnp.float32),
            jnp.zeros((B, H, dh), jnp.float32),
            jnp.zeros((B, H), jnp.float32))
    _, hc = lax.scan(step, init, (qc, kc, vc, ic, lfc))
    return jnp.moveaxis(hc, 0, 2).reshape(B, H, S, dh)


def mlstm_branch(xm, z, conv_w, conv_b, wq, wk, wv, w_gate, b_gate, norm_g, skip, w_down):
    B, S, W = xm.shape
    H, dh = MLSTM_HEADS, MLSTM_HEAD_DIM
    xc = jax.nn.silu(causal_dwconv(xm, conv_w, conv_b))

    def blockdiag(t, w):
        tb = t.reshape(B, S, W // QKV_BLOCK, QKV_BLOCK)
        return jnp.einsum("bsnc,ncd->bsnd", tb, w.astype(t.dtype)).reshape(B, S, W)

    q = blockdiag(xc, wq)
    k = blockdiag(xc, wk)
    v = blockdiag(xm, wv)
    gates = (jnp.concatenate([q, k, v], axis=-1) @ w_gate + b_gate).astype(jnp.float32)
    i_pre = jnp.transpose(gates[..., :H], (0, 2, 1))
    f_pre = jnp.transpose(gates[..., H:], (0, 2, 1))

    def heads(t):
        return jnp.transpose(t.reshape(B, S, H, dh), (0, 2, 1, 3)).astype(jnp.float32)

    h = mlstm_chunkwise(heads(q), heads(k), heads(v), i_pre, f_pre)
    mu = jnp.mean(h, -1, keepdims=True)
    var = jnp.mean(jnp.square(h - mu), -1, keepdims=True)
    hn = (h - mu) * lax.rsqrt(var + LN_EPS)
    hn = jnp.transpose(hn, (0, 2, 1, 3)).reshape(B, S, W).astype(xm.dtype) * norm_g
    out = (hn + skip * xc) * jax.nn.silu(z)
    return out @ w_down


def conformer_conv_branch(a, g, dw_w, dw_b, ln_g, ln_b, w_pw, b_pw):
    u = a * jax.nn.sigmoid(g)
    u = causal_dwconv(u, dw_w, dw_b)
    u = jax.nn.silu(layernorm(u, ln_g, ln_b))
    return u @ w_pw + b_pw


def cross_attention(h, mem_n, wq, wk, wv, wo):
    B, S, _ = h.shape
    N = mem_n.shape[1]
    q = (h @ wq).reshape(B, S, XATTN_HEADS, XATTN_HEAD_DIM)
    k = (mem_n @ wk).reshape(B, N, XATTN_HEADS, XATTN_HEAD_DIM)
    v = (mem_n @ wv).reshape(B, N, XATTN_HEADS, XATTN_HEAD_DIM)
    s = jnp.einsum("bshd,bnhd->bhsn", q, k).astype(jnp.float32) * (XATTN_HEAD_DIM ** -0.5)
    p = jax.nn.softmax(s, axis=-1).astype(v.dtype)
    o = jnp.einsum("bhsn,bnhd->bshd", p, v).reshape(B, S, D_MODEL)
    return o @ wo


def conv_ffn(h, w_up, b_up, dw_w, dw_b, w_down):
    u = causal_dwconv(h @ w_up + b_up, dw_w, dw_b)
    gate, val = u[..., :D_FF], u[..., D_FF:]
    return (jax.nn.silu(gate) * val) @ w_down


def setup_inputs(seed: int = 0) -> dict:
    key = jax.random.key(seed)
    ks = iter(jax.random.split(key, 48))

    def nrm(shape, scale):
        return jax.random.normal(next(ks), shape, jnp.float32) * scale

    L, D, W, CW, F, H = DEPTH, D_MODEL, MLSTM_WIDTH, CONV_WIDTH, D_FF, MLSTM_HEADS
    nb = W // QKV_BLOCK
    f_bias = jnp.broadcast_to(jnp.linspace(3.0, 6.0, H, dtype=jnp.float32), (L, H)) + nrm((L, H), 0.01)
    ml_b_gate = jnp.concatenate([nrm((L, H), 0.1), f_bias], axis=-1)
    return {
        "x": nrm((BATCH, SEQ, D), 1.0),
        "mem": nrm((BATCH, N_MEM, D), 1.0),
        "norm_mix": 1.0 + nrm((L, D), 0.02),
        "w_in": nrm((L, D, IN_COLS), D ** -0.5),
        "b_in": nrm((L, IN_COLS), 0.02),
        "ml_conv_w": nrm((L, MLSTM_CONV, W), MLSTM_CONV ** -0.5),
        "ml_conv_b": nrm((L, W), 0.02),
        "ml_wq": nrm((L, nb, QKV_BLOCK, QKV_BLOCK), QKV_BLOCK ** -0.5),
        "ml_wk": nrm((L, nb, QKV_BLOCK, QKV_BLOCK), QKV_BLOCK ** -0.5),
        "ml_wv": nrm((L, nb, QKV_BLOCK, QKV_BLOCK), QKV_BLOCK ** -0.5),
        "ml_w_gate": nrm((L, 3 * W, 2 * H), (3 * W) ** -0.5),
        "ml_b_gate": ml_b_gate,
        "ml_norm_g": 1.0 + nrm((L, W), 0.02),
        "ml_skip": 1.0 + nrm((L, W), 0.02),
        "ml_w_down": nrm((L, W, D), W ** -0.5),
        "cv_dw_w": nrm((L, CONV_KERNEL, CW), CONV_KERNEL ** -0.5),
        "cv_dw_b": nrm((L, CW), 0.02),
        "cv_ln_g": 1.0 + nrm((L, CW), 0.02),
        "cv_ln_b": nrm((L, CW), 0.02),
        "cv_w_pw": nrm((L, CW, D), CW ** -0.5),
        "cv_b_pw": nrm((L, D), 0.02),
        "w_out": nrm((L, D, D), D ** -0.5),
        "norm_x": 1.0 + nrm((L, D), 0.02),
        "norm_mem": 1.0 + nrm((L, D), 0.02),
        "xa_wq": nrm((L, D, D), D ** -0.5),
        "xa_wk": nrm((L, D, D), D ** -0.5),
        "xa_wv": nrm((L, D, D), D ** -0.5),
        "xa_wo": nrm((L, D, D), D ** -0.5),
        "norm_ffn": 1.0 + nrm((L, D), 0.02),
        "ffn_w_up": nrm((L, D, 2 * F), D ** -0.5),
        "ffn_b_up": nrm((L, 2 * F), 0.02),
        "ffn_dw_w": nrm((L, FFN_CONV, 2 * F), FFN_CONV ** -0.5),
        "ffn_dw_b": nrm((L, 2 * F), 0.02),
        "ffn_w_down": nrm((L, F, D), F ** -0.5),
        "final_norm": 1.0 + nrm((D,), 0.02),
    }


def reference(x, mem, norm_mix, w_in, b_in, ml_conv_w, ml_conv_b, ml_wq, ml_wk, ml_wv,
              ml_w_gate, ml_b_gate, ml_norm_g, ml_skip, ml_w_down, cv_dw_w, cv_dw_b,
              cv_ln_g, cv_ln_b, cv_w_pw, cv_b_pw, w_out, norm_x, norm_mem, xa_wq, xa_wk,
              xa_wv, xa_wo, norm_ffn, ffn_w_up, ffn_b_up, ffn_dw_w, ffn_dw_b, ffn_w_down,
              final_norm):
    W, CW, D = MLSTM_WIDTH, CONV_WIDTH, D_MODEL
    o1, o2, o3, o4, o5 = W, 2 * W, 2 * W + CW, 2 * W + 2 * CW, 2 * W + 2 * CW + D
    for l in range(DEPTH):
        h = rmsnorm(x, norm_mix[l])
        p = h @ w_in[l] + b_in[l]
        y_m = mlstm_branch(p[..., :o1], p[..., o1:o2], ml_conv_w[l], ml_conv_b[l],
                           ml_wq[l], ml_wk[l], ml_wv[l], ml_w_gate[l], ml_b_gate[l],
                           ml_norm_g[l], ml_skip[l], ml_w_down[l])
        y_c = conformer_conv_branch(p[..., o2:o3], p[..., o3:o4], cv_dw_w[l], cv_dw_b[l],
                                    cv_ln_g[l], cv_ln_b[l], cv_w_pw[l], cv_b_pw[l])
        merged = jax.nn.sigmoid(p[..., o4:o5]) * y_m + jax.nn.sigmoid(p[..., o5:]) * y_c
        x = x + merged @ w_out[l]
        x = x + cross_attention(rmsnorm(x, norm_x[l]), rmsnorm(mem, norm_mem[l]),
                                xa_wq[l], xa_wk[l], xa_wv[l], xa_wo[l])
        x = x + conv_ffn(rmsnorm(x, norm_ffn[l]), ffn_w_up[l], ffn_b_up[l],
                         ffn_dw_w[l], ffn_dw_b[l], ffn_w_down[l])
    return rmsnorm(x, final_norm)
```

```python
import functools

import jax
import jax.numpy as jnp
from jax import lax
from jax.experimental import pallas as pl
from jax.experimental.pallas import tpu as pltpu

F32 = jnp.float32
BF16 = jnp.bfloat16

MLSTM_HEADS = 4
QKV_BLOCK = 4
XATTN_HEADS = 4
RMS_EPS = 1e-6
LN_EPS = 1e-5

LANES = 128
SUBLANES = 8
MLSTM_CHUNK = 256
SEQ_TILE_MIXER = 256
SEQ_TILE_XATTN = 512
SEQ_TILE_FFN = 512
VMEM_LIMIT_BYTES = 56 * 1024 * 1024


def _const_spec(shape):
    nd = len(shape)
    return pl.BlockSpec(shape, lambda *_: (0,) * nd, pipeline_mode=pl.Buffered(1))


def _rmsnorm(x, g):
    return x * lax.rsqrt(jnp.mean(x * x, axis=-1, keepdims=True) + RMS_EPS) * g


def _sigmoid(x):
    return 1.0 / (1.0 + jnp.exp(-x))


def _silu(x):
    return x * _sigmoid(x)


def _log_sigmoid(x):
    return jnp.minimum(x, 0.0) - jnp.log(1.0 + jnp.exp(-jnp.abs(x)))


def _dot(a, b):
    return jnp.dot(a, b, preferred_element_type=F32)


def _blockdiag_pair(t, coef_refs):
    width = t.shape[-1]
    outs = [None] * len(coef_refs)
    for delta in range(-(QKV_BLOCK - 1), QKV_BLOCK):
        shifted = t if delta == 0 else pltpu.roll(t, (-delta) % width, axis=1)
        for n, cref in enumerate(coef_refs):
            term = shifted * cref[delta + QKV_BLOCK - 1:delta + QKV_BLOCK, :]
            outs[n] = term if outs[n] is None else outs[n] + term
    return outs


def _mlstm_chunk(qc, kc, vc, g_col, c_st, n_st, m_st):
    L, W = qc.shape
    H = MLSTM_HEADS
    dh = W // H
    row = lax.broadcasted_iota(jnp.int32, (L, L), 0)
    col = lax.broadcasted_iota(jnp.int32, (L, L), 1)
    causal = col <= row
    tril = causal.astype(BF16)

    lf = _log_sigmoid(g_col)
    lf_hi = lf.astype(BF16)
    r1 = lf - lf_hi.astype(F32)
    lf_mid = r1.astype(BF16)
    lf_lo = (r1 - lf_mid.astype(F32)).astype(BF16)
    b_all = _dot(tril, lf_hi) + _dot(tril, lf_mid) + _dot(tril, lf_lo)

    lane = lax.broadcasted_iota(jnp.int32, (L, LANES), 1)
    packed_row = jnp.where(lane < H, g_col, b_all).T

    outs = []
    for h in range(H):
        sl = slice(h * dh, (h + 1) * dh)
        b_c = b_all[:, H + h:H + h + 1]
        i_c = g_col[:, h:h + 1]
        i_r = packed_row[h:h + 1, :]
        b_r = packed_row[H + h:H + h + 1, :]
        r_r = i_r - b_r
        m_prev = m_st[h][0:1, 0:1]

        dmat = jnp.where(causal, b_c + r_r, -jnp.inf)
        a_c = b_c + m_prev
        m_c = jnp.maximum(a_c, jnp.max(dmat, axis=-1, keepdims=True))
        w_inter = jnp.exp(a_c - m_c)
        pmat = jnp.exp(dmat - m_c)

        q_f = qc[:, sl]
        k_f = kc[:, sl]
        q_h = q_f.astype(BF16)
        k_h = k_f.astype(BF16)
        v_h = vc[:, sl].astype(BF16)
        qk = lax.dot_general(q_h, k_h, (((1,), (1,)), ((), ())), preferred_element_type=F32)
        s = qk * pmat
        c_prev = c_st[h]
        n_prev = n_st[h][0:1, :]
        num = _dot(s.astype(BF16), v_h) + w_inter * _dot(q_h, c_prev.astype(BF16))
        den = (jnp.sum(s, axis=-1, keepdims=True)
               + w_inter * jnp.sum(q_f * n_prev, axis=-1, keepdims=True))
        hh = num / jnp.maximum(jnp.abs(den), jnp.exp(-m_c))

        mu = jnp.mean(hh, axis=-1, keepdims=True)
        cen = hh - mu
        var = jnp.mean(cen * cen, axis=-1, keepdims=True)
        outs.append(cen * lax.rsqrt(var + LN_EPS))

        b_end = b_c[L - 1:L, :]
        m_new = jnp.maximum(b_end + m_prev, jnp.max(b_end + r_r, axis=-1, keepdims=True))
        decay = jnp.exp(b_end + m_prev - m_new)
        wk = jnp.exp(b_end - b_c + i_c - m_new) * k_f
        kv = lax.dot_general(wk.astype(BF16), v_h, (((0,), (0,)), ((), ())),
                             preferred_element_type=F32)
        c_st[h] = decay * c_prev + kv
        n_st[h] = jnp.broadcast_to(decay * n_prev + jnp.sum(wk, axis=0, keepdims=True),
                                   (SUBLANES, dh))
        m_st[h] = jnp.broadcast_to(m_new, (SUBLANES, LANES))
    return jnp.concatenate(outs, axis=-1)


def _mixer_kernel(x_ref, g_ref, win_ref, bin_ref, mcw_ref, mcb_ref, cq_ref, ck_ref, cv_ref,
                  wg_ref, bg_ref, ng_ref, sk_ref, wd_ref, dww_ref, dwb_ref, lng_ref, lnb_ref,
                  wpw_ref, bpw_ref, wout_ref, o_ref,
                  xm_buf, u_buf, c_st, n_st, m_st, *, mconv, cconv):
    T, D = x_ref.shape
    W = wd_ref.shape[0]
    CW = wpw_ref.shape[0]
    XH = SUBLANES
    UH = 4 * SUBLANES
    o1, o2, o3, o4, o5 = W, 2 * W, 2 * W + CW, 2 * W + 2 * CW, 2 * W + 2 * CW + D

    @pl.when(pl.program_id(1) == 0)
    def _():
        xm_buf[0:XH, :] = jnp.zeros((XH, W), F32)
        u_buf[0:UH, :] = jnp.zeros((UH, CW), F32)
        c_st[...] = jnp.zeros_like(c_st)
        n_st[...] = jnp.zeros_like(n_st)
        m_st[...] = jnp.zeros_like(m_st)

    x = x_ref[...]
    hb = _rmsnorm(x, g_ref[...]).astype(BF16)

    def proj(c0, c1):
        return _dot(hb, win_ref[:, c0:c1]) + bin_ref[:, c0:c1]

    xm = proj(0, o1)
    xm_buf[XH:XH + T, :] = xm
    acc = xm * mcw_ref[mconv - 1:mconv, :] + mcb_ref[...]
    for k in range(mconv - 1):
        off = XH - (mconv - 1) + k
        acc = acc + xm_buf[off:off + T, :] * mcw_ref[k:k + 1, :]
    xm_buf[0:XH, :] = xm_buf[T:T + XH, :]
    xc = _silu(acc)

    q, k = _blockdiag_pair(xc, (cq_ref, ck_ref))
    (v,) = _blockdiag_pair(xm, (cv_ref,))
    g_col = (_dot(q.astype(BF16), wg_ref[0:W, :]) + _dot(k.astype(BF16), wg_ref[W:2 * W, :])
             + _dot(v.astype(BF16), wg_ref[2 * W:3 * W, :]) + bg_ref[...])
    ks = k * ((W // MLSTM_HEADS) ** -0.5)

    L = min(MLSTM_CHUNK, T)
    hn_parts = []
    for c in range(T // L):
        rs = slice(c * L, (c + 1) * L)
        hn_parts.append(_mlstm_chunk(q[rs], ks[rs], v[rs], g_col[rs], c_st, n_st, m_st))
    hn = hn_parts[0] if len(hn_parts) == 1 else jnp.concatenate(hn_parts, axis=0)
    z = proj(o1, o2)
    ml_out = (hn * ng_ref[...] + sk_ref[...] * xc) * _silu(z)
    y_m = _dot(ml_out.astype(BF16), wd_ref[...])

    u = proj(o2, o3) * _sigmoid(proj(o3, o4))
    u_buf[UH:UH + T, :] = u
    acc = u * dww_ref[cconv - 1:cconv, :] + dwb_ref[...]
    for k in range(cconv - 1):
        off = UH - (cconv - 1) + k
        acc = acc + u_buf[off:off + T, :] * dww_ref[k:k + 1, :]
    u_buf[0:UH, :] = u_buf[T:T + UH, :]
    mu = jnp.mean(acc, axis=-1, keepdims=True)
    cen = acc - mu
    var = jnp.mean(cen * cen, axis=-1, keepdims=True)
    cv = _silu(cen * lax.rsqrt(var + LN_EPS) * lng_ref[...] + lnb_ref[...])
    y_c = _dot(cv.astype(BF16), wpw_ref[...]) + bpw_ref[...]

    merged = _sigmoid(proj(o4, o5)) * y_m + _sigmoid(proj(o5, o5 + D)) * y_c
    o_ref[...] = x + _dot(merged.astype(BF16), wout_ref[...])


def _blockdiag_coef(w):
    nb, bs, _ = w.shape
    rows = []
    d = jnp.arange(bs)
    for delta in range(-(bs - 1), bs):
        c = d + delta
        valid = (c >= 0) & (c < bs)
        coef = jnp.where(valid[None, :], w[:, jnp.clip(c, 0, bs - 1), d], 0.0)
        rows.append(coef.reshape(nb * bs))
    rows.append(jnp.zeros((nb * bs,), w.dtype))
    return jnp.stack(rows, axis=0)


def _row(v):
    return v.reshape(1, -1).astype(F32)


def _mixer(x, p, tile):
    B, S, D = x.shape
    W = p["ml_w_down"].shape[0]
    CW = p["cv_w_pw"].shape[0]
    H = MLSTM_HEADS
    dh = W // H
    mconv = p["ml_conv_w"].shape[0]
    cconv = p["cv_dw_w"].shape[0]
    T = min(tile, S)
    in_cols = p["w_in"].shape[1]

    wg = jnp.zeros((3 * W, LANES), F32).at[:, :2 * H].set(p["ml_w_gate"]).astype(BF16)
    bg = jnp.zeros((1, LANES), F32).at[0, :2 * H].set(p["ml_b_gate"])
    dww = jnp.zeros((4 * SUBLANES, CW), F32).at[:cconv].set(p["cv_dw_w"])
    mcw = jnp.zeros((SUBLANES, W), F32).at[:mconv].set(p["ml_conv_w"])

    args = [
        x, _row(p["norm_mix"]), p["w_in"].astype(BF16), _row(p["b_in"]), mcw, _row(p["ml_conv_b"]),
        _blockdiag_coef(p["ml_wq"]), _blockdiag_coef(p["ml_wk"]), _blockdiag_coef(p["ml_wv"]),
        wg, bg, _row(p["ml_norm_g"]), _row(p["ml_skip"]), p["ml_w_down"].astype(BF16),
        dww, _row(p["cv_dw_b"]), _row(p["cv_ln_g"]), _row(p["cv_ln_b"]),
        p["cv_w_pw"].astype(BF16), _row(p["cv_b_pw"]), p["w_out"].astype(BF16),
    ]
    x_spec = pl.BlockSpec((None, T, D), lambda b, s: (b, s, 0))
    in_specs = [x_spec] + [_const_spec(a.shape) for a in args[1:]]
    return pl.pallas_call(
        functools.partial(_mixer_kernel, mconv=mconv, cconv=cconv),
        grid=(B, S // T),
        in_specs=in_specs,
        out_specs=x_spec,
        out_shape=jax.ShapeDtypeStruct((B, S, D), F32),
        scratch_shapes=[
            pltpu.VMEM((SUBLANES + T, W), F32),
            pltpu.VMEM((4 * SUBLANES + T, CW), F32),
            pltpu.VMEM((H, dh, dh), F32),
            pltpu.VMEM((H, SUBLANES, dh), F32),
            pltpu.VMEM((H, SUBLANES, LANES), F32),
        ],
        compiler_params=pltpu.CompilerParams(
            dimension_semantics=("arbitrary", "arbitrary"),
            vmem_limit_bytes=VMEM_LIMIT_BYTES),
        name="mixer",
    )(*args)


def _memkv_kernel(mem_ref, g_ref, wk_ref, wv_ref, kt_ref, v_ref):
    mn = _rmsnorm(mem_ref[...], g_ref[...]).astype(BF16)
    kt_ref[...] = _dot(mn, wk_ref[...]).T.astype(BF16)
    v_ref[...] = _dot(mn, wv_ref[...]).astype(BF16)


def _memkv(mem, g, wk, wv):
    B, N, D = mem.shape
    return pl.pallas_call(
        _memkv_kernel,
        grid=(B,),
        in_specs=[pl.BlockSpec((None, N, D), lambda b: (b, 0, 0)), _const_spec((1, D)),
                  _const_spec((D, D)), _const_spec((D, D))],
        out_specs=[pl.BlockSpec((None, D, N), lambda b: (b, 0, 0)),
                   pl.BlockSpec((None, N, D), lambda b: (b, 0, 0))],
        out_shape=[jax.ShapeDtypeStruct((B, D, N), BF16), jax.ShapeDtypeStruct((B, N, D), BF16)],
        compiler_params=pltpu.CompilerParams(
            dimension_semantics=("arbitrary",), vmem_limit_bytes=VMEM_LIMIT_BYTES),
        name="memkv",
    )(mem, _row(g), wk.astype(BF16), wv.astype(BF16))


def _xattn_kernel(x_ref, g_ref, wq_ref, kt_ref, v_ref, wo_ref, o_ref):
    T, D = x_ref.shape
    dh = D // XATTN_HEADS
    x = x_ref[...]
    hb = _rmsnorm(x, g_ref[...]).astype(BF16)
    q = (_dot(hb, wq_ref[...]) * (dh ** -0.5)).astype(BF16)
    outs = []
    for h in range(XATTN_HEADS):
        sl = slice(h * dh, (h + 1) * dh)
        s = _dot(q[:, sl], kt_ref[sl, :])
        e = jnp.exp(s - jnp.max(s, axis=-1, keepdims=True))
        pr = e / jnp.sum(e, axis=-1, keepdims=True)
        outs.append(_dot(pr.astype(BF16), v_ref[:, sl]))
    o = jnp.concatenate(outs, axis=-1).astype(BF16)
    o_ref[...] = x + _dot(o, wo_ref[...])


def _xattn(x, kt, v, g, wq, wo, tile):
    B, S, D = x.shape
    N = v.shape[1]
    T = min(tile, S)
    x_spec = pl.BlockSpec((None, T, D), lambda b, s: (b, s, 0))
    return pl.pallas_call(
        _xattn_kernel,
        grid=(B, S // T),
        in_specs=[x_spec, _const_spec((1, D)), _const_spec((D, D)),
                  pl.BlockSpec((None, D, N), lambda b, s: (b, 0, 0)),
                  pl.BlockSpec((None, N, D), lambda b, s: (b, 0, 0)),
                  _const_spec((D, D))],
        out_specs=x_spec,
        out_shape=jax.ShapeDtypeStruct((B, S, D), F32),
        compiler_params=pltpu.CompilerParams(
            dimension_semantics=("arbitrary", "arbitrary"), vmem_limit_bytes=VMEM_LIMIT_BYTES),
        name="xattn",
    )(x, _row(g), wq.astype(BF16), kt, v, wo.astype(BF16))


def _ffn_kernel(x_ref, g_ref, wup_ref, bup_ref, dww_ref, dwb_ref, wdn_ref, fg_ref, o_ref,
                u_buf, *, kconv, col_chunk, final_norm):
    T, D = x_ref.shape
    F = wdn_ref.shape[0]
    UH = SUBLANES

    @pl.when(pl.program_id(1) == 0)
    def _():
        u_buf[0:UH, :] = jnp.zeros((UH, 2 * F), F32)

    x = x_ref[...]
    hb = _rmsnorm(x, g_ref[...]).astype(BF16)

    def conv_cols(c0, c1):
        u = _dot(hb, wup_ref[:, c0:c1]) + bup_ref[:, c0:c1]
        u_buf[UH:UH + T, c0:c1] = u
        acc = u * dww_ref[kconv - 1:kconv, c0:c1] + dwb_ref[:, c0:c1]
        for k in range(kconv - 1):
            off = UH - (kconv - 1) + k
            acc = acc + u_buf[off:off + T, c0:c1] * dww_ref[k:k + 1, c0:c1]
        u_buf[0:UH, c0:c1] = u_buf[T:T + UH, c0:c1]
        return acc

    y = x
    for c0 in range(0, F, col_chunk):
        c1 = c0 + col_chunk
        act = _silu(conv_cols(c0, c1)) * conv_cols(F + c0, F + c1)
        y = y + _dot(act.astype(BF16), wdn_ref[c0:c1, :])
    if final_norm:
        y = _rmsnorm(y, fg_ref[...])
    o_ref[...] = y


def _ffn(x, p, final_g, tile, final_norm):
    B, S, D = x.shape
    F = p["ffn_w_down"].shape[0]
    kconv = p["ffn_dw_w"].shape[0]
    T = min(tile, S)
    col_chunk = F // 2
    assert col_chunk % LANES == 0
    dww = jnp.zeros((SUBLANES, 2 * F), F32).at[:kconv].set(p["ffn_dw_w"])
    args = [x, _row(p["norm_ffn"]), p["ffn_w_up"].astype(BF16), _row(p["ffn_b_up"]), dww,
            _row(p["ffn_dw_b"]), p["ffn_w_down"].astype(BF16), _row(final_g)]
    x_spec = pl.BlockSpec((None, T, D), lambda b, s: (b, s, 0))
    return pl.pallas_call(
        functools.partial(_ffn_kernel, kconv=kconv, col_chunk=col_chunk, final_norm=final_norm),
        grid=(B, S // T),
        in_specs=[x_spec] + [_const_spec(a.shape) for a in args[1:]],
        out_specs=x_spec,
        out_shape=jax.ShapeDtypeStruct((B, S, D), F32),
        scratch_shapes=[pltpu.VMEM((SUBLANES + T, 2 * F), F32)],
        compiler_params=pltpu.CompilerParams(
            dimension_semantics=("arbitrary", "arbitrary"), vmem_limit_bytes=VMEM_LIMIT_BYTES),
        name="convffn",
    )(*args)


_LAYER_PARAMS = (
    "norm_mix", "w_in", "b_in", "ml_conv_w", "ml_conv_b", "ml_wq", "ml_wk", "ml_wv", "ml_w_gate",
    "ml_b_gate", "ml_norm_g", "ml_skip", "ml_w_down", "cv_dw_w", "cv_dw_b", "cv_ln_g", "cv_ln_b",
    "cv_w_pw", "cv_b_pw", "w_out", "norm_x", "norm_mem", "xa_wq", "xa_wk", "xa_wv", "xa_wo",
    "norm_ffn", "ffn_w_up", "ffn_b_up", "ffn_dw_w", "ffn_dw_b", "ffn_w_down")


def kernel(x, mem, norm_mix, w_in, b_in, ml_conv_w, ml_conv_b, ml_wq, ml_wk, ml_wv, ml_w_gate, ml_b_gate, ml_norm_g, ml_skip, ml_w_down, cv_dw_w, cv_dw_b, cv_ln_g, cv_ln_b, cv_w_pw, cv_b_pw, w_out, norm_x, norm_mem, xa_wq, xa_wk, xa_wv, xa_wo, norm_ffn, ffn_w_up, ffn_b_up, ffn_dw_w, ffn_dw_b, ffn_w_down, final_norm):
    stacked = dict(zip(_LAYER_PARAMS, (
        norm_mix, w_in, b_in, ml_conv_w, ml_conv_b, ml_wq, ml_wk, ml_wv, ml_w_gate, ml_b_gate,
        ml_norm_g, ml_skip, ml_w_down, cv_dw_w, cv_dw_b, cv_ln_g, cv_ln_b, cv_w_pw, cv_b_pw,
        w_out, norm_x, norm_mem, xa_wq, xa_wk, xa_wv, xa_wo, norm_ffn, ffn_w_up, ffn_b_up,
        ffn_dw_w, ffn_dw_b, ffn_w_down)))
    depth = w_in.shape[0]
    for l in range(depth):
        p = {name: val[l] for name, val in stacked.items()}
        x = _mixer(x, p, SEQ_TILE_MIXER)
        kt, v = _memkv(mem, p["norm_mem"], p["xa_wk"], p["xa_wv"])
        x = _xattn(x, kt, v, p["norm_x"], p["xa_wq"], p["xa_wo"], SEQ_TILE_XATTN)
        x = _ffn(x, p, final_norm, SEQ_TILE_FFN, final_norm=(l == depth - 1))
    return x
```

```python
import functools

import jax
import jax.numpy as jnp
from jax import lax
from jax.experimental import pallas as pl
from jax.experimental.pallas import tpu as pltpu

F32 = jnp.float32
BF16 = jnp.bfloat16

MLSTM_HEADS = 4
XATTN_HEADS = 4
RMS_EPS = 1e-6
LN_EPS = 1e-5

LANES = 128
SUBLANES = 8
MXU_DIM = 256
MLSTM_CHUNK = 256
TIME_BLOCK = 256
SEQ_TILE_MIXER = TIME_BLOCK
SEQ_TILE_XATTN = 512
SEQ_TILE_FFN = 512
VMEM_LIMIT_BYTES = 56 * 1024 * 1024


def _const_spec(shape):
    nd = len(shape)
    return pl.BlockSpec(shape, lambda *_: (0,) * nd, pipeline_mode=pl.Buffered(1))


def _rmsnorm(x, g):
    return x * lax.rsqrt(jnp.mean(x * x, axis=-1, keepdims=True) + RMS_EPS) * g


def _sigmoid(x):
    return 1.0 / (1.0 + jnp.exp(-x))


def _silu(x):
    return x * _sigmoid(x)


def _log_sigmoid(x):
    return jnp.minimum(x, 0.0) - jnp.log(1.0 + jnp.exp(-jnp.abs(x)))


def _dot(a, b):
    return jnp.dot(a, b, preferred_element_type=F32)


def _cat(parts, axis):
    return parts[0] if len(parts) == 1 else jnp.concatenate(parts, axis=axis)


def _time_index(idx):
    return (idx % SUBLANES) * (TIME_BLOCK // SUBLANES) + idx // SUBLANES


def _conv_stage(u, ebuf, hist, col0, taps):
    P, cc = u.shape
    H = taps - 1
    assert P == TIME_BLOCK and SUBLANES * H <= P and cc % LANES == 0
    cs = slice(col0, col0 + cc)
    tail = u[P - SUBLANES * H:, :]
    rot = pltpu.roll(tail.reshape(H, SUBLANES, cc), 1, axis=1).reshape(SUBLANES * H, cc)
    first = lax.broadcasted_iota(jnp.int32, (SUBLANES * H, cc), 0) % SUBLANES == 0
    ebuf[0:SUBLANES * H, cs] = jnp.where(first, hist[:, cs], rot)
    hist[:, cs] = rot
    ebuf[SUBLANES * H:SUBLANES * H + P, cs] = u


def _conv_lanes(ebuf, w_ref, b_ref, c, taps):
    H = taps - 1
    P = TIME_BLOCK
    sl = slice(c, c + LANES)
    acc = ebuf[SUBLANES * H:SUBLANES * H + P, sl] * w_ref[H:H + 1, sl] + b_ref[:, sl]
    for lag in range(1, H + 1):
        r0 = SUBLANES * (H - lag)
        acc = acc + ebuf[r0:r0 + P, sl] * w_ref[H - lag:H - lag + 1, sl]
    return acc


def _perm_causal_conv(u, w_ref, b_ref, ebuf, hist, col0, taps):
    _conv_stage(u, ebuf, hist, col0, taps)
    return _cat([_conv_lanes(ebuf, w_ref, b_ref, c, taps)
                 for c in range(col0, col0 + u.shape[1], LANES)], axis=-1)


def _mlstm_gates(g_col):
    L = g_col.shape[0]
    H = MLSTM_HEADS
    row = _time_index(lax.broadcasted_iota(jnp.int32, (L, L), 0))
    col = _time_index(lax.broadcasted_iota(jnp.int32, (L, L), 1))
    causal = col <= row
    tril = causal.astype(BF16)
    lf = _log_sigmoid(g_col)
    lf_hi = lf.astype(BF16)
    r1 = lf - lf_hi.astype(F32)
    lf_mid = r1.astype(BF16)
    lf_lo = (r1 - lf_mid.astype(F32)).astype(BF16)
    b_all = _dot(tril, lf_hi) + _dot(tril, lf_mid) + _dot(tril, lf_lo)
    lane = lax.broadcasted_iota(jnp.int32, (L, LANES), 1)
    packed_row = jnp.where(lane < H, g_col, b_all).T
    return causal, b_all, packed_row


def _mlstm_head(h, q_f, k_f, v_f, g_col, causal, b_all, packed_row, c_st, n_st, m_st):
    L, dh = q_f.shape
    H = MLSTM_HEADS
    b_c = b_all[:, H + h:H + h + 1]
    i_c = g_col[:, h:h + 1]
    i_r = packed_row[h:h + 1, :]
    b_r = packed_row[H + h:H + h + 1, :]
    r_r = i_r - b_r
    m_prev = m_st[h][0:1, 0:1]

    dmat = jnp.where(causal, b_c + r_r, -jnp.inf)
    a_c = b_c + m_prev
    m_c = jnp.maximum(a_c, jnp.max(dmat, axis=-1, keepdims=True))
    w_inter = jnp.exp(a_c - m_c)
    pmat = jnp.exp(dmat - m_c)

    q_h = q_f.astype(BF16)
    k_h = k_f.astype(BF16)
    v_h = v_f.astype(BF16)
    qk = lax.dot_general(q_h, k_h, (((1,), (1,)), ((), ())), preferred_element_type=F32)
    s = qk * pmat
    c_prev = c_st[h]
    n_prev = n_st[h][0:1, :]
    num = _dot(s.astype(BF16), v_h) + w_inter * _dot(q_h, c_prev.astype(BF16))
    den = (jnp.sum(s, axis=-1, keepdims=True)
           + w_inter * jnp.sum(q_f * n_prev, axis=-1, keepdims=True))
    hh = num / jnp.maximum(jnp.abs(den), jnp.exp(-m_c))

    mu = jnp.mean(hh, axis=-1, keepdims=True)
    cen = hh - mu
    var = jnp.mean(cen * cen, axis=-1, keepdims=True)
    out = cen * lax.rsqrt(var + LN_EPS)

    b_end = b_c[L - 1:L, :]
    m_new = jnp.maximum(b_end + m_prev, jnp.max(b_end + r_r, axis=-1, keepdims=True))
    decay = jnp.exp(b_end + m_prev - m_new)
    wk = jnp.exp(b_end - b_c + i_c - m_new) * k_f
    kv = lax.dot_general(wk.astype(BF16), v_h, (((0,), (0,)), ((), ())),
                         preferred_element_type=F32)
    c_st[h] = decay * c_prev + kv
    n_st[h] = jnp.broadcast_to(decay * n_prev + jnp.sum(wk, axis=0, keepdims=True),
                               (SUBLANES, dh))
    m_st[h] = jnp.broadcast_to(m_new, (SUBLANES, LANES))
    return out


def _mixer_kernel(x_ref, g_ref, win_ref, bin_ref, mcw_ref, mcb_ref, bdq_ref, bdk_ref, bdv_ref,
                  wg_ref, bg_ref, ng_ref, sk_ref, wd_ref, dww_ref, dwb_ref, lng_ref, lnb_ref,
                  wpw_ref, bpw_ref, wout_ref, o_ref,
                  xm_buf, xm_hist, u_buf, u_hist, c_st, n_st, m_st, *, mconv, cconv):
    T, D = x_ref.shape
    W = wd_ref.shape[0]
    CW = wpw_ref.shape[0]
    H = MLSTM_HEADS
    dh = W // H
    assert T == TIME_BLOCK == MLSTM_CHUNK and dh == MXU_DIM
    o1, o2, o3, o4, o5 = W, 2 * W, 2 * W + CW, 2 * W + 2 * CW, 2 * W + 2 * CW + D

    @pl.when(pl.program_id(1) == 0)
    def _():
        xm_hist[...] = jnp.zeros_like(xm_hist)
        u_hist[...] = jnp.zeros_like(u_hist)
        c_st[...] = jnp.zeros_like(c_st)
        n_st[...] = jnp.zeros_like(n_st)
        m_st[...] = jnp.zeros_like(m_st)

    x = x_ref[...]
    hb = _rmsnorm(x, g_ref[...]).astype(BF16)

    def proj(c0, c1):
        return _dot(hb, win_ref[:, c0:c1]) + bin_ref[:, c0:c1]

    xm = proj(0, o1)
    xc = _silu(_perm_causal_conv(xm, mcw_ref, mcb_ref, xm_buf, xm_hist, 0, mconv))
    _conv_stage(proj(o2, o3) * _sigmoid(proj(o3, o4)), u_buf, u_hist, 0, cconv)

    xc_b = xc.astype(BF16)
    xm_b = xm.astype(BF16)
    hs = [slice(h * dh, (h + 1) * dh) for h in range(H)]
    q = [_dot(xc_b[:, hs[h]], bdq_ref[h]) for h in range(H)]
    k = [_dot(xc_b[:, hs[h]], bdk_ref[h]) for h in range(H)]
    v = [_dot(xm_b[:, hs[h]], bdv_ref[h]) for h in range(H)]
    g_col = bg_ref[...]
    for n, part in enumerate((q, k, v)):
        for h in range(H):
            g_col = g_col + _dot(part[h].astype(BF16), wg_ref[n * W + h * dh:n * W + (h + 1) * dh, :])
    causal, b_all, packed_row = _mlstm_gates(g_col)

    lane_groups = list(range(0, CW, LANES))
    per_head = -(-len(lane_groups) // H)
    hn, conv, z, pgm, pgc = [], [], [], [], []
    for h in range(H):
        hn.append(_mlstm_head(h, q[h], k[h] * (dh ** -0.5), v[h], g_col, causal, b_all,
                              packed_row, c_st, n_st, m_st))
        for c in lane_groups[h * per_head:(h + 1) * per_head]:
            conv.append(_conv_lanes(u_buf, dww_ref, dwb_ref, c, cconv))
        z.append(proj(o1 + h * dh, o1 + (h + 1) * dh))
        pgm.append(proj(o4 + h * dh, o4 + (h + 1) * dh))
        pgc.append(proj(o5 + h * dh, o5 + (h + 1) * dh))
    hn = _cat(hn, axis=-1)
    acc = _cat(conv, axis=-1)

    ml_out = (hn * ng_ref[...] + sk_ref[...] * xc) * _silu(_cat(z, axis=-1))
    y_m = _dot(ml_out.astype(BF16), wd_ref[...])

    mu = jnp.mean(acc, axis=-1, keepdims=True)
    cen = acc - mu
    var = jnp.mean(cen * cen, axis=-1, keepdims=True)
    cv = _silu(cen * lax.rsqrt(var + LN_EPS) * lng_ref[...] + lnb_ref[...])
    y_c = _dot(cv.astype(BF16), wpw_ref[...]) + bpw_ref[...]

    merged = _sigmoid(_cat(pgm, axis=-1)) * y_m + _sigmoid(_cat(pgc, axis=-1)) * y_c
    o_ref[...] = x + _dot(merged.astype(BF16), wout_ref[...])


def _blockdiag_tiles(w):
    nb, bs, _ = w.shape
    per = MXU_DIM // bs
    wt = w.reshape(nb // per, per, bs, bs)
    eye = jnp.eye(per, dtype=w.dtype)
    dense = wt[:, :, :, None, :] * eye[None, :, None, :, None]
    return dense.reshape(nb // per, MXU_DIM, MXU_DIM)


def _row(v):
    return v.reshape(1, -1).astype(F32)


def _pad_rows(w):
    pad = -w.shape[0] % SUBLANES
    return jnp.pad(w.astype(F32), ((0, pad), (0, 0)))


def _mixer(x, p, tile):
    B, S, D = x.shape
    W = p["ml_w_down"].shape[0]
    CW = p["cv_w_pw"].shape[0]
    H = MLSTM_HEADS
    dh = W // H
    mconv = p["ml_conv_w"].shape[0]
    cconv = p["cv_dw_w"].shape[0]
    T = tile
    assert S % T == 0

    wg = jnp.zeros((3 * W, LANES), F32).at[:, :2 * H].set(p["ml_w_gate"]).astype(BF16)
    bg = jnp.zeros((1, LANES), F32).at[0, :2 * H].set(p["ml_b_gate"])

    args = [
        x, _row(p["norm_mix"]), p["w_in"].astype(BF16), _row(p["b_in"]),
        _pad_rows(p["ml_conv_w"]), _row(p["ml_conv_b"]),
        _blockdiag_tiles(p["ml_wq"]).astype(BF16), _blockdiag_tiles(p["ml_wk"]).astype(BF16),
        _blockdiag_tiles(p["ml_wv"]).astype(BF16),
        wg, bg, _row(p["ml_norm_g"]), _row(p["ml_skip"]), p["ml_w_down"].astype(BF16),
        _pad_rows(p["cv_dw_w"]), _row(p["cv_dw_b"]), _row(p["cv_ln_g"]), _row(p["cv_ln_b"]),
        p["cv_w_pw"].astype(BF16), _row(p["cv_b_pw"]), p["w_out"].astype(BF16),
    ]
    x_spec = pl.BlockSpec((None, T, D), lambda b, s: (b, s, 0))
    in_specs = [x_spec] + [_const_spec(a.shape) for a in args[1:]]
    return pl.pallas_call(
        functools.partial(_mixer_kernel, mconv=mconv, cconv=cconv),
        grid=(B, S // T),
        in_specs=in_specs,
        out_specs=x_spec,
        out_shape=jax.ShapeDtypeStruct((B, S, D), F32),
        scratch_shapes=[
            pltpu.VMEM((SUBLANES * (mconv - 1) + T, W), F32),
            pltpu.VMEM((SUBLANES * (mconv - 1), W), F32),
            pltpu.VMEM((SUBLANES * (cconv - 1) + T, CW), F32),
            pltpu.VMEM((SUBLANES * (cconv - 1), CW), F32),
            pltpu.VMEM((H, dh, dh), F32),
            pltpu.VMEM((H, SUBLANES, dh), F32),
            pltpu.VMEM((H, SUBLANES, LANES), F32),
        ],
        compiler_params=pltpu.CompilerParams(
            dimension_semantics=("arbitrary", "arbitrary"),
            vmem_limit_bytes=VMEM_LIMIT_BYTES),
        name="mixer",
    )(*args)


def _memkv_kernel(mem_ref, g_ref, wk_ref, wv_ref, kt_ref, v_ref):
    mn = _rmsnorm(mem_ref[...], g_ref[...]).astype(BF16)
    kt_ref[...] = _dot(mn, wk_ref[...]).T.astype(BF16)
    v_ref[...] = _dot(mn, wv_ref[...]).astype(BF16)


def _memkv(mem, g, wk, wv):
    B, N, D = mem.shape
    return pl.pallas_call(
        _memkv_kernel,
        grid=(B,),
        in_specs=[pl.BlockSpec((None, N, D), lambda b: (b, 0, 0)), _const_spec((1, D)),
                  _const_spec((D, D)), _const_spec((D, D))],
        out_specs=[pl.BlockSpec((None, D, N), lambda b: (b, 0, 0)),
                   pl.BlockSpec((None, N, D), lambda b: (b, 0, 0))],
        out_shape=[jax.ShapeDtypeStruct((B, D, N), BF16), jax.ShapeDtypeStruct((B, N, D), BF16)],
        compiler_params=pltpu.CompilerParams(
            dimension_semantics=("arbitrary",), vmem_limit_bytes=VMEM_LIMIT_BYTES),
        name="memkv",
    )(mem, _row(g), wk.astype(BF16), wv.astype(BF16))


def _xattn_kernel(x_ref, g_ref, wq_ref, kt_ref, v_ref, wo_ref, o_ref):
    T, D = x_ref.shape
    dh = D // XATTN_HEADS
    x = x_ref[...]
    hb = _rmsnorm(x, g_ref[...]).astype(BF16)
    q = (_dot(hb, wq_ref[...]) * (dh ** -0.5)).astype(BF16)
    outs = []
    for h in range(XATTN_HEADS):
        sl = slice(h * dh, (h + 1) * dh)
        s = _dot(q[:, sl], kt_ref[sl, :])
        e = jnp.exp(s - jnp.max(s, axis=-1, keepdims=True))
        pr = e / jnp.sum(e, axis=-1, keepdims=True)
        outs.append(_dot(pr.astype(BF16), v_ref[:, sl]))
    o = jnp.concatenate(outs, axis=-1).astype(BF16)
    o_ref[...] = x + _dot(o, wo_ref[...])


def _xattn(x, kt, v, g, wq, wo, tile):
    B, S, D = x.shape
    N = v.shape[1]
    T = tile
    assert S % T == 0
    x_spec = pl.BlockSpec((None, T, D), lambda b, s: (b, s, 0))
    return pl.pallas_call(
        _xattn_kernel,
        grid=(B, S // T),
        in_specs=[x_spec, _const_spec((1, D)), _const_spec((D, D)),
                  pl.BlockSpec((None, D, N), lambda b, s: (b, 0, 0)),
                  pl.BlockSpec((None, N, D), lambda b, s: (b, 0, 0)),
                  _const_spec((D, D))],
        out_specs=x_spec,
        out_shape=jax.ShapeDtypeStruct((B, S, D), F32),
        compiler_params=pltpu.CompilerParams(
            dimension_semantics=("arbitrary", "arbitrary"), vmem_limit_bytes=VMEM_LIMIT_BYTES),
        name="xattn",
    )(x, _row(g), wq.astype(BF16), kt, v, wo.astype(BF16))


def _ffn_kernel(x_ref, g_ref, wup_ref, bup_ref, dww_ref, dwb_ref, wdn_ref, fg_ref, o_ref,
                u_buf, u_hist, *, kconv, col_chunk, final_norm):
    T, D = x_ref.shape
    F = wdn_ref.shape[0]
    P = TIME_BLOCK

    @pl.when(pl.program_id(1) == 0)
    def _():
        u_hist[...] = jnp.zeros_like(u_hist)

    x = x_ref[...]
    hb = _rmsnorm(x, g_ref[...]).astype(BF16)

    def conv_cols(c0, c1):
        u = _dot(hb, wup_ref[:, c0:c1]) + bup_ref[:, c0:c1]
        return _cat([_perm_causal_conv(u[r:r + P], dww_ref, dwb_ref, u_buf, u_hist, c0, kconv)
                     for r in range(0, T, P)], axis=0)

    y = x
    for c0 in range(0, F, col_chunk):
        c1 = min(c0 + col_chunk, F)
        act = _silu(conv_cols(c0, c1)) * conv_cols(F + c0, F + c1)
        y = y + _dot(act.astype(BF16), wdn_ref[c0:c1, :])
    if final_norm:
        y = _rmsnorm(y, fg_ref[...])
    o_ref[...] = y


def _ffn(x, p, final_g, tile, final_norm):
    B, S, D = x.shape
    F = p["ffn_w_down"].shape[0]
    kconv = p["ffn_dw_w"].shape[0]
    T = tile
    col_chunk = -(-F // (2 * MXU_DIM)) * MXU_DIM
    assert S % T == 0 and T % TIME_BLOCK == 0 and F % LANES == 0
    args = [x, _row(p["norm_ffn"]), p["ffn_w_up"].astype(BF16), _row(p["ffn_b_up"]),
            _pad_rows(p["ffn_dw_w"]),
            _row(p["ffn_dw_b"]), p["ffn_w_down"].astype(BF16), _row(final_g)]
    x_spec = pl.BlockSpec((None, T, D), lambda b, s: (b, s, 0))
    return pl.pallas_call(
        functools.partial(_ffn_kernel, kconv=kconv, col_chunk=col_chunk, final_norm=final_norm),
        grid=(B, S // T),
        in_specs=[x_spec] + [_const_spec(a.shape) for a in args[1:]],
        out_specs=x_spec,
        out_shape=jax.ShapeDtypeStruct((B, S, D), F32),
        scratch_shapes=[pltpu.VMEM((SUBLANES * (kconv - 1) + TIME_BLOCK, 2 * F), F32),
                        pltpu.VMEM((SUBLANES * (kconv - 1), 2 * F), F32)],
        compiler_params=pltpu.CompilerParams(
            dimension_semantics=("arbitrary", "arbitrary"), vmem_limit_bytes=VMEM_LIMIT_BYTES),
        name="convffn",
    )(*args)


_LAYER_PARAMS = (
    "norm_mix", "w_in", "b_in", "ml_conv_w", "ml_conv_b", "ml_wq", "ml_wk", "ml_wv", "ml_w_gate",
    "ml_b_gate", "ml_norm_g", "ml_skip", "ml_w_down", "cv_dw_w", "cv_dw_b", "cv_ln_g", "cv_ln_b",
    "cv_w_pw", "cv_b_pw", "w_out", "norm_x", "norm_mem", "xa_wq", "xa_wk", "xa_wv", "xa_wo",
    "norm_ffn", "ffn_w_up", "ffn_b_up", "ffn_dw_w", "ffn_dw_b", "ffn_w_down")


def kernel(x, mem, norm_mix, w_in, b_in, ml_conv_w, ml_conv_b, ml_wq, ml_wk, ml_wv, ml_w_gate, ml_b_gate, ml_norm_g, ml_skip, ml_w_down, cv_dw_w, cv_dw_b, cv_ln_g, cv_ln_b, cv_w_pw, cv_b_pw, w_out, norm_x, norm_mem, xa_wq, xa_wk, xa_wv, xa_wo, norm_ffn, ffn_w_up, ffn_b_up, ffn_dw_w, ffn_dw_b, ffn_w_down, final_norm):
    stacked = dict(zip(_LAYER_PARAMS, (
        norm_mix, w_in, b_in, ml_conv_w, ml_conv_b, ml_wq, ml_wk, ml_wv, ml_w_gate, ml_b_gate,
        ml_norm_g, ml_skip, ml_w_down, cv_dw_w, cv_dw_b, cv_ln_g, cv_ln_b, cv_w_pw, cv_b_pw,
        w_out, norm_x, norm_mem, xa_wq, xa_wk, xa_wv, xa_wo, norm_ffn, ffn_w_up, ffn_b_up,
        ffn_dw_w, ffn_dw_b, ffn_w_down)))
    depth = w_in.shape[0]
    B, S, D = x.shape
    G = TIME_BLOCK // SUBLANES
    assert S % TIME_BLOCK == 0
    x = x.reshape(B, S // TIME_BLOCK, SUBLANES, G, D).swapaxes(2, 3).reshape(B, S, D)
    for l in range(depth):
        p = {name: val[l] for name, val in stacked.items()}
        x = _mixer(x, p, SEQ_TILE_MIXER)
        kt, v = _memkv(mem, p["norm_mem"], p["xa_wk"], p["xa_wv"])
        x = _xattn(x, kt, v, p["norm_x"], p["xa_wq"], p["xa_wo"], SEQ_TILE_XATTN)
        x = _ffn(x, p, final_norm, SEQ_TILE_FFN, final_norm=(l == depth - 1))
    return x.reshape(B, S // TIME_BLOCK, G, SUBLANES, D).swapaxes(2, 3).reshape(B, S, D)
```

```python
import functools

import jax
import jax.numpy as jnp
from jax import lax
from jax.experimental import pallas as pl
from jax.experimental.pallas import tpu as pltpu

F32 = jnp.float32
BF16 = jnp.bfloat16

MLSTM_HEADS = 4
XATTN_HEADS = 4
RMS_EPS = 1e-6
LN_EPS = 1e-5

LANES = 128
SUBLANES = 8
MXU_DIM = 256
MLSTM_CHUNK = 256
TIME_BLOCK = 256
SEQ_TILE_MIXER = TIME_BLOCK
SEQ_TILE_XATTN = 1024
SEQ_TILE_FFN = 512
VMEM_LIMIT_BYTES = 56 * 1024 * 1024


def _const_spec(shape):
    nd = len(shape)
    return pl.BlockSpec(shape, lambda *_: (0,) * nd, pipeline_mode=pl.Buffered(1))


def _rmsnorm(x, g):
    return x * lax.rsqrt(jnp.mean(x * x, axis=-1, keepdims=True) + RMS_EPS) * g


def _sigmoid(x):
    return 1.0 / (1.0 + jnp.exp(-x))


def _silu(x):
    return x * _sigmoid(x)


def _log_sigmoid(x):
    return jnp.minimum(x, 0.0) - jnp.log(1.0 + jnp.exp(-jnp.abs(x)))


def _dot(a, b):
    return jnp.dot(a, b, preferred_element_type=F32)


def _cat(parts, axis):
    return parts[0] if len(parts) == 1 else jnp.concatenate(parts, axis=axis)


def _time_index(idx):
    return (idx % SUBLANES) * (TIME_BLOCK // SUBLANES) + idx // SUBLANES


def _conv_stage(u, ebuf, hist, col0, taps):
    P, cc = u.shape
    H = taps - 1
    assert P == TIME_BLOCK and SUBLANES * H <= P and cc % LANES == 0
    cs = slice(col0, col0 + cc)
    tail = u[P - SUBLANES * H:, :]
    rot = pltpu.roll(tail.reshape(H, SUBLANES, cc), 1, axis=1).reshape(SUBLANES * H, cc)
    first = lax.broadcasted_iota(jnp.int32, (SUBLANES * H, cc), 0) % SUBLANES == 0
    ebuf[0:SUBLANES * H, cs] = jnp.where(first, hist[:, cs], rot)
    hist[:, cs] = rot
    ebuf[SUBLANES * H:SUBLANES * H + P, cs] = u


def _conv_lanes(ebuf, w_ref, b_ref, c, taps):
    H = taps - 1
    P = TIME_BLOCK
    sl = slice(c, c + LANES)
    acc = ebuf[SUBLANES * H:SUBLANES * H + P, sl] * w_ref[H:H + 1, sl] + b_ref[:, sl]
    for lag in range(1, H + 1):
        r0 = SUBLANES * (H - lag)
        acc = acc + ebuf[r0:r0 + P, sl] * w_ref[H - lag:H - lag + 1, sl]
    return acc


def _perm_causal_conv(u, w_ref, b_ref, ebuf, hist, col0, taps):
    _conv_stage(u, ebuf, hist, col0, taps)
    return _cat([_conv_lanes(ebuf, w_ref, b_ref, c, taps)
                 for c in range(col0, col0 + u.shape[1], LANES)], axis=-1)


def _mlstm_gates(g_col):
    L = g_col.shape[0]
    H = MLSTM_HEADS
    row = _time_index(lax.broadcasted_iota(jnp.int32, (L, L), 0))
    col = _time_index(lax.broadcasted_iota(jnp.int32, (L, L), 1))
    causal = col <= row
    tril = causal.astype(BF16)
    lf = _log_sigmoid(g_col)
    lf_hi = lf.astype(BF16)
    r1 = lf - lf_hi.astype(F32)
    lf_mid = r1.astype(BF16)
    lf_lo = (r1 - lf_mid.astype(F32)).astype(BF16)
    b_all = _dot(tril, lf_hi) + _dot(tril, lf_mid) + _dot(tril, lf_lo)
    lane = lax.broadcasted_iota(jnp.int32, (L, LANES), 1)
    packed_row = jnp.where(lane < H, g_col, b_all).T
    return causal, b_all, packed_row


def _mlstm_head(h, q_f, k_f, v_f, g_col, causal, b_all, packed_row, c_st, n_st, m_st):
    L, dh = q_f.shape
    H = MLSTM_HEADS
    b_c = b_all[:, H + h:H + h + 1]
    i_c = g_col[:, h:h + 1]
    i_r = packed_row[h:h + 1, :]
    b_r = packed_row[H + h:H + h + 1, :]
    r_r = i_r - b_r
    m_prev = m_st[h][0:1, 0:1]

    dmat = jnp.where(causal, b_c + r_r, -jnp.inf)
    a_c = b_c + m_prev
    m_c = jnp.maximum(a_c, jnp.max(dmat, axis=-1, keepdims=True))
    w_inter = jnp.exp(a_c - m_c)
    pmat = jnp.exp(dmat - m_c)

    q_h = q_f.astype(BF16)
    k_h = k_f.astype(BF16)
    v_h = v_f.astype(BF16)
    qk = lax.dot_general(q_h, k_h, (((1,), (1,)), ((), ())), preferred_element_type=F32)
    s = qk * pmat
    c_prev = c_st[h]
    n_prev = n_st[h][0:1, :]
    num = _dot(s.astype(BF16), v_h) + w_inter * _dot(q_h, c_prev.astype(BF16))
    den = (jnp.sum(s, axis=-1, keepdims=True)
           + w_inter * jnp.sum(q_f * n_prev, axis=-1, keepdims=True))
    hh = num / jnp.maximum(jnp.abs(den), jnp.exp(-m_c))

    mu = jnp.mean(hh, axis=-1, keepdims=True)
    cen = hh - mu
    var = jnp.mean(cen * cen, axis=-1, keepdims=True)
    out = cen * lax.rsqrt(var + LN_EPS)

    b_end = b_c[L - 1:L, :]
    m_new = jnp.maximum(b_end + m_prev, jnp.max(b_end + r_r, axis=-1, keepdims=True))
    decay = jnp.exp(b_end + m_prev - m_new)
    wk = jnp.exp(b_end - b_c + i_c - m_new) * k_f
    kv = lax.dot_general(wk.astype(BF16), v_h, (((0,), (0,)), ((), ())),
                         preferred_element_type=F32)
    c_st[h] = decay * c_prev + kv
    n_st[h] = jnp.broadcast_to(decay * n_prev + jnp.sum(wk, axis=0, keepdims=True),
                               (SUBLANES, dh))
    m_st[h] = jnp.broadcast_to(m_new, (SUBLANES, LANES))
    return out


def _mixer_kernel(x_ref, g_ref, win_ref, bin_ref, mcw_ref, mcb_ref, bdq_ref, bdk_ref, bdv_ref,
                  wg_ref, bg_ref, ng_ref, sk_ref, wd_ref, dww_ref, dwb_ref, lng_ref, lnb_ref,
                  wpw_ref, bpw_ref, wout_ref, o_ref,
                  xm_buf, xm_hist, u_buf, u_hist, c_st, n_st, m_st, *, mconv, cconv):
    T, D = x_ref.shape
    W = wd_ref.shape[0]
    CW = wpw_ref.shape[0]
    H = MLSTM_HEADS
    dh = W // H
    assert T == TIME_BLOCK == MLSTM_CHUNK and dh == MXU_DIM and D == W == CW
    o1, o2, o3, o4, o5 = W, 2 * W, 2 * W + CW, 2 * W + 2 * CW, 2 * W + 2 * CW + D

    @pl.when(pl.program_id(1) == 0)
    def _():
        xm_hist[...] = jnp.zeros_like(xm_hist)
        u_hist[...] = jnp.zeros_like(u_hist)
        c_st[...] = jnp.zeros_like(c_st)
        n_st[...] = jnp.zeros_like(n_st)
        m_st[...] = jnp.zeros_like(m_st)

    x = x_ref[...]
    hb = _rmsnorm(x, g_ref[...]).astype(BF16)

    def proj(c0, c1):
        return _dot(hb, win_ref[:, c0:c1]) + bin_ref[:, c0:c1]

    xm = proj(0, o1)
    xc = _silu(_perm_causal_conv(xm, mcw_ref, mcb_ref, xm_buf, xm_hist, 0, mconv))
    _conv_stage(proj(o2, o3) * _sigmoid(proj(o3, o4)), u_buf, u_hist, 0, cconv)

    xc_b = xc.astype(BF16)
    xm_b = xm.astype(BF16)
    hs = [slice(h * dh, (h + 1) * dh) for h in range(H)]
    q = [_dot(xc_b[:, hs[h]], bdq_ref[h]) for h in range(H)]
    k = [_dot(xc_b[:, hs[h]], bdk_ref[h]) for h in range(H)]
    v = [_dot(xm_b[:, hs[h]], bdv_ref[h]) for h in range(H)]
    g_col = bg_ref[...]
    for n, part in enumerate((q, k, v)):
        for h in range(H):
            g_col = g_col + _dot(part[h].astype(BF16), wg_ref[n * W + h * dh:n * W + (h + 1) * dh, :])
    causal, b_all, packed_row = _mlstm_gates(g_col)

    lane_groups = list(range(0, CW, LANES))
    per_head = -(-len(lane_groups) // H)
    hn, conv, z, pgm, pgc = [], [], [], [], []
    for h in range(H):
        hn.append(_mlstm_head(h, q[h], k[h] * (dh ** -0.5), v[h], g_col, causal, b_all,
                              packed_row, c_st, n_st, m_st))
        for c in lane_groups[h * per_head:(h + 1) * per_head]:
            conv.append(_conv_lanes(u_buf, dww_ref, dwb_ref, c, cconv))
        z.append(proj(o1 + h * dh, o1 + (h + 1) * dh))
        pgm.append(proj(o4 + h * dh, o4 + (h + 1) * dh))
        pgc.append(proj(o5 + h * dh, o5 + (h + 1) * dh))
    hn = _cat(hn, axis=-1)
    acc = _cat(conv, axis=-1)

    ml_out = (hn * ng_ref[...] + sk_ref[...] * xc) * _silu(_cat(z, axis=-1))
    y_m = _dot(ml_out.astype(BF16), wd_ref[...])

    mu = jnp.mean(acc, axis=-1, keepdims=True)
    cen = acc - mu
    var = jnp.mean(cen * cen, axis=-1, keepdims=True)
    cv = _silu(cen * lax.rsqrt(var + LN_EPS) * lng_ref[...] + lnb_ref[...])
    y_c = _dot(cv.astype(BF16), wpw_ref[...]) + bpw_ref[...]

    merged = _sigmoid(_cat(pgm, axis=-1)) * y_m + _sigmoid(_cat(pgc, axis=-1)) * y_c
    o_ref[...] = x + _dot(merged.astype(BF16), wout_ref[...])


def _blockdiag_tiles(w):
    nb, bs, _ = w.shape
    col = jnp.arange(MXU_DIM)
    expand = (col[None, :] % bs == jnp.arange(bs)[:, None]).astype(w.dtype)
    spread = jnp.dot(w.reshape(nb * bs, bs), expand, precision=lax.Precision.HIGHEST)
    spread = spread.reshape(nb * bs // MXU_DIM, MXU_DIM, MXU_DIM)
    on_block = col[:, None] // bs == col[None, :] // bs
    return jnp.where(on_block[None], spread, 0.0)


def _row(v):
    return v.reshape(1, -1).astype(F32)


def _pad_rows(w):
    pad = -w.shape[0] % SUBLANES
    return jnp.pad(w.astype(F32), ((0, pad), (0, 0)))


def _mixer(x, p, tile):
    B, S, D = x.shape
    W = p["ml_w_down"].shape[0]
    CW = p["cv_w_pw"].shape[0]
    H = MLSTM_HEADS
    dh = W // H
    mconv = p["ml_conv_w"].shape[0]
    cconv = p["cv_dw_w"].shape[0]
    T = tile
    assert S % T == 0

    wg = jnp.zeros((3 * W, LANES), F32).at[:, :2 * H].set(p["ml_w_gate"]).astype(BF16)
    bg = jnp.zeros((1, LANES), F32).at[0, :2 * H].set(p["ml_b_gate"])

    args = [
        x, _row(p["norm_mix"]), p["w_in"].astype(BF16), _row(p["b_in"]),
        _pad_rows(p["ml_conv_w"]), _row(p["ml_conv_b"]),
        _blockdiag_tiles(p["ml_wq"]).astype(BF16), _blockdiag_tiles(p["ml_wk"]).astype(BF16),
        _blockdiag_tiles(p["ml_wv"]).astype(BF16),
        wg, bg, _row(p["ml_norm_g"]), _row(p["ml_skip"]), p["ml_w_down"].astype(BF16),
        _pad_rows(p["cv_dw_w"]), _row(p["cv_dw_b"]), _row(p["cv_ln_g"]), _row(p["cv_ln_b"]),
        p["cv_w_pw"].astype(BF16), _row(p["cv_b_pw"]), p["w_out"].astype(BF16),
    ]
    x_spec = pl.BlockSpec((None, T, D), lambda b, s: (b, s, 0))
    in_specs = [x_spec] + [_const_spec(a.shape) for a in args[1:]]
    return pl.pallas_call(
        functools.partial(_mixer_kernel, mconv=mconv, cconv=cconv),
        grid=(B, S // T),
        in_specs=in_specs,
        out_specs=x_spec,
        out_shape=jax.ShapeDtypeStruct((B, S, D), F32),
        scratch_shapes=[
            pltpu.VMEM((SUBLANES * (mconv - 1) + T, W), F32),
            pltpu.VMEM((SUBLANES * (mconv - 1), W), F32),
            pltpu.VMEM((SUBLANES * (cconv - 1) + T, CW), F32),
            pltpu.VMEM((SUBLANES * (cconv - 1), CW), F32),
            pltpu.VMEM((H, dh, dh), F32),
            pltpu.VMEM((H, SUBLANES, dh), F32),
            pltpu.VMEM((H, SUBLANES, LANES), F32),
        ],
        compiler_params=pltpu.CompilerParams(
            dimension_semantics=("arbitrary", "arbitrary"),
            vmem_limit_bytes=VMEM_LIMIT_BYTES),
        name="mixer",
    )(*args)


def _memkv_kernel(mem_ref, g_ref, wk_ref, wv_ref, kt_ref, v_ref):
    mn = _rmsnorm(mem_ref[...], g_ref[...]).astype(BF16)
    kt_ref[...] = _dot(mn, wk_ref[...]).T.astype(BF16)
    v_ref[...] = _dot(mn, wv_ref[...]).astype(BF16)


def _memkv(mem, g, wk, wv):
    B, N, D = mem.shape
    return pl.pallas_call(
        _memkv_kernel,
        grid=(B,),
        in_specs=[pl.BlockSpec((None, N, D), lambda b: (b, 0, 0)), _const_spec((1, D)),
                  _const_spec((D, D)), _const_spec((D, D))],
        out_specs=[pl.BlockSpec((None, D, N), lambda b: (b, 0, 0)),
                   pl.BlockSpec((None, N, D), lambda b: (b, 0, 0))],
        out_shape=[jax.ShapeDtypeStruct((B, D, N), BF16), jax.ShapeDtypeStruct((B, N, D), BF16)],
        compiler_params=pltpu.CompilerParams(
            dimension_semantics=("arbitrary",), vmem_limit_bytes=VMEM_LIMIT_BYTES),
        name="memkv",
    )(mem, _row(g), wk.astype(BF16), wv.astype(BF16))


def _xattn_kernel(x_ref, g_ref, wq_ref, kt_ref, v_ref, wo_ref, o_ref):
    T, D = x_ref.shape
    dh = D // XATTN_HEADS
    x = x_ref[...]
    hb = _rmsnorm(x, g_ref[...]).astype(BF16)
    q = (_dot(hb, wq_ref[...]) * (dh ** -0.5)).astype(BF16)
    outs = []
    for h in range(XATTN_HEADS):
        sl = slice(h * dh, (h + 1) * dh)
        s = _dot(q[:, sl], kt_ref[sl, :])
        e = jnp.exp(s - jnp.max(s, axis=-1, keepdims=True))
        pr = e / jnp.sum(e, axis=-1, keepdims=True)
        outs.append(_dot(pr.astype(BF16), v_ref[:, sl]))
    o = jnp.concatenate(outs, axis=-1).astype(BF16)
    o_ref[...] = x + _dot(o, wo_ref[...])


def _xattn(x, kt, v, g, wq, wo, tile):
    B, S, D = x.shape
    N = v.shape[1]
    T = tile
    assert S % T == 0
    x_spec = pl.BlockSpec((None, T, D), lambda b, s: (b, s, 0))
    return pl.pallas_call(
        _xattn_kernel,
        grid=(B, S // T),
        in_specs=[x_spec, _const_spec((1, D)), _const_spec((D, D)),
                  pl.BlockSpec((None, D, N), lambda b, s: (b, 0, 0)),
                  pl.BlockSpec((None, N, D), lambda b, s: (b, 0, 0)),
                  _const_spec((D, D))],
        out_specs=x_spec,
        out_shape=jax.ShapeDtypeStruct((B, S, D), F32),
        compiler_params=pltpu.CompilerParams(
            dimension_semantics=("arbitrary", "arbitrary"), vmem_limit_bytes=VMEM_LIMIT_BYTES),
        name="xattn",
    )(x, _row(g), wq.astype(BF16), kt, v, wo.astype(BF16))


def _ffn_kernel(x_ref, g_ref, wup_ref, bup_ref, dww_ref, dwb_ref, wdn_ref, fg_ref, o_ref,
                u_buf, u_hist, *, kconv, col_chunk, final_norm):
    T, D = x_ref.shape
    F = wdn_ref.shape[0]
    P = TIME_BLOCK

    @pl.when(pl.program_id(1) == 0)
    def _():
        u_hist[...] = jnp.zeros_like(u_hist)

    x = x_ref[...]
    hb = _rmsnorm(x, g_ref[...]).astype(BF16)

    def conv_cols(c0, c1):
        u = _dot(hb, wup_ref[:, c0:c1]) + bup_ref[:, c0:c1]
        return _cat([_perm_causal_conv(u[r:r + P], dww_ref, dwb_ref, u_buf, u_hist, c0, kconv)
                     for r in range(0, T, P)], axis=0)

    y = x
    for c0 in range(0, F, col_chunk):
        c1 = min(c0 + col_chunk, F)
        act = _silu(conv_cols(c0, c1)) * conv_cols(F + c0, F + c1)
        y = y + _dot(act.astype(BF16), wdn_ref[c0:c1, :])
    if final_norm:
        y = _rmsnorm(y, fg_ref[...])
    o_ref[...] = y


def _ffn(x, p, final_g, tile, final_norm):
    B, S, D = x.shape
    F = p["ffn_w_down"].shape[0]
    kconv = p["ffn_dw_w"].shape[0]
    T = tile
    col_chunk = -(-F // (2 * MXU_DIM)) * MXU_DIM
    assert S % T == 0 and T % TIME_BLOCK == 0 and F % LANES == 0
    args = [x, _row(p["norm_ffn"]), p["ffn_w_up"].astype(BF16), _row(p["ffn_b_up"]),
            _pad_rows(p["ffn_dw_w"]),
            _row(p["ffn_dw_b"]), p["ffn_w_down"].astype(BF16), _row(final_g)]
    x_spec = pl.BlockSpec((None, T, D), lambda b, s: (b, s, 0))
    return pl.pallas_call(
        functools.partial(_ffn_kernel, kconv=kconv, col_chunk=col_chunk, final_norm=final_norm),
        grid=(B, S // T),
        in_specs=[x_spec] + [_const_spec(a.shape) for a in args[1:]],
        out_specs=x_spec,
        out_shape=jax.ShapeDtypeStruct((B, S, D), F32),
        scratch_shapes=[pltpu.VMEM((SUBLANES * (kconv - 1) + TIME_BLOCK, 2 * F), F32),
                        pltpu.VMEM((SUBLANES * (kconv - 1), 2 * F), F32)],
        compiler_params=pltpu.CompilerParams(
            dimension_semantics=("arbitrary", "arbitrary"), vmem_limit_bytes=VMEM_LIMIT_BYTES),
        name="convffn",
    )(*args)


_LAYER_PARAMS = (
    "norm_mix", "w_in", "b_in", "ml_conv_w", "ml_conv_b", "ml_wq", "ml_wk", "ml_wv", "ml_w_gate",
    "ml_b_gate", "ml_norm_g", "ml_skip", "ml_w_down", "cv_dw_w", "cv_dw_b", "cv_ln_g", "cv_ln_b",
    "cv_w_pw", "cv_b_pw", "w_out", "norm_x", "norm_mem", "xa_wq", "xa_wk", "xa_wv", "xa_wo",
    "norm_ffn", "ffn_w_up", "ffn_b_up", "ffn_dw_w", "ffn_dw_b", "ffn_w_down")


def kernel(x, mem, norm_mix, w_in, b_in, ml_conv_w, ml_conv_b, ml_wq, ml_wk, ml_wv, ml_w_gate, ml_b_gate, ml_norm_g, ml_skip, ml_w_down, cv_dw_w, cv_dw_b, cv_ln_g, cv_ln_b, cv_w_pw, cv_b_pw, w_out, norm_x, norm_mem, xa_wq, xa_wk, xa_wv, xa_wo, norm_ffn, ffn_w_up, ffn_b_up, ffn_dw_w, ffn_dw_b, ffn_w_down, final_norm):
    stacked = dict(zip(_LAYER_PARAMS, (
        norm_mix, w_in, b_in, ml_conv_w, ml_conv_b, ml_wq, ml_wk, ml_wv, ml_w_gate, ml_b_gate,
        ml_norm_g, ml_skip, ml_w_down, cv_dw_w, cv_dw_b, cv_ln_g, cv_ln_b, cv_w_pw, cv_b_pw,
        w_out, norm_x, norm_mem, xa_wq, xa_wk, xa_wv, xa_wo, norm_ffn, ffn_w_up, ffn_b_up,
        ffn_dw_w, ffn_dw_b, ffn_w_down)))
    depth = w_in.shape[0]
    B, S, D = x.shape
    G = TIME_BLOCK // SUBLANES
    assert S % TIME_BLOCK == 0
    x = x.reshape(B, S // TIME_BLOCK, SUBLANES, G, D).swapaxes(2, 3).reshape(B, S, D)
    for l in range(depth):
        p = {name: val[l] for name, val in stacked.items()}
        x = _mixer(x, p, SEQ_TILE_MIXER)
        kt, v = _memkv(mem, p["norm_mem"], p["xa_wk"], p["xa_wv"])
        x = _xattn(x, kt, v, p["norm_x"], p["xa_wq"], p["xa_wo"], SEQ_TILE_XATTN)
        x = _ffn(x, p, final_norm, SEQ_TILE_FFN, final_norm=(l == depth - 1))
    return x.reshape(B, S // TIME_BLOCK, G, SUBLANES, D).swapaxes(2, 3).reshape(B, S, D)
```

```python
import functools

import jax
import jax.numpy as jnp
from jax import lax
from jax.experimental import pallas as pl
from jax.experimental.pallas import tpu as pltpu

F32 = jnp.float32
BF16 = jnp.bfloat16

MLSTM_HEADS = 4
XATTN_HEADS = 4
RMS_EPS = 1e-6
LN_EPS = 1e-5

LANES = 128
SUBLANES = 8
MXU_DIM = 256
MLSTM_CHUNK = 256
TIME_BLOCK = 256
SEQ_TILE_MIXER = TIME_BLOCK
SEQ_TILE_XATTN = 1024
SEQ_TILE_FFN = 512
VMEM_LIMIT_BYTES = 56 * 1024 * 1024


def _const_spec(shape):
    nd = len(shape)
    return pl.BlockSpec(shape, lambda *_: (0,) * nd, pipeline_mode=pl.Buffered(1))


class _LayerSlice:
    def __init__(self, stacked, layer):
        self.stacked = stacked
        self.layer = layer

    @property
    def shape(self):
        return tuple(self.stacked.shape[1:])


def _operand(a):
    return a.stacked if isinstance(a, _LayerSlice) else a


def _weight_spec(a):
    if not isinstance(a, _LayerSlice):
        return _const_spec(a.shape)
    index = (a.layer,) + (0,) * len(a.shape)
    return pl.BlockSpec((None,) + a.shape, lambda *_: index, pipeline_mode=pl.Buffered(1))


def _rmsnorm(x, g):
    return x * lax.rsqrt(jnp.mean(x * x, axis=-1, keepdims=True) + RMS_EPS) * g


def _sigmoid(x):
    return 1.0 / (1.0 + jnp.exp(-x))


def _silu(x):
    return x * _sigmoid(x)


def _log_sigmoid(x):
    return jnp.minimum(x, 0.0) - jnp.log(1.0 + jnp.exp(-jnp.abs(x)))


def _dot(a, b):
    return jnp.dot(a, b, preferred_element_type=F32)


def _cat(parts, axis):
    return parts[0] if len(parts) == 1 else jnp.concatenate(parts, axis=axis)


def _time_index(idx):
    return (idx % SUBLANES) * (TIME_BLOCK // SUBLANES) + idx // SUBLANES


def _conv_stage(u, ebuf, hist, col0, taps):
    P, cc = u.shape
    H = taps - 1
    assert P == TIME_BLOCK and SUBLANES * H <= P and cc % LANES == 0
    cs = slice(col0, col0 + cc)
    tail = u[P - SUBLANES * H:, :]
    rot = pltpu.roll(tail.reshape(H, SUBLANES, cc), 1, axis=1).reshape(SUBLANES * H, cc)
    first = lax.broadcasted_iota(jnp.int32, (SUBLANES * H, cc), 0) % SUBLANES == 0
    ebuf[0:SUBLANES * H, cs] = jnp.where(first, hist[:, cs], rot)
    hist[:, cs] = rot
    ebuf[SUBLANES * H:SUBLANES * H + P, cs] = u


def _conv_lanes(ebuf, w_ref, b_ref, c, taps):
    H = taps - 1
    P = TIME_BLOCK
    sl = slice(c, c + LANES)
    acc = ebuf[SUBLANES * H:SUBLANES * H + P, sl] * w_ref[H:H + 1, sl] + b_ref[:, sl]
    for lag in range(1, H + 1):
        r0 = SUBLANES * (H - lag)
        acc = acc + ebuf[r0:r0 + P, sl] * w_ref[H - lag:H - lag + 1, sl]
    return acc


def _perm_causal_conv(u, w_ref, b_ref, ebuf, hist, col0, taps):
    _conv_stage(u, ebuf, hist, col0, taps)
    return _cat([_conv_lanes(ebuf, w_ref, b_ref, c, taps)
                 for c in range(col0, col0 + u.shape[1], LANES)], axis=-1)


def _mlstm_gates(g_col):
    L = g_col.shape[0]
    H = MLSTM_HEADS
    row = _time_index(lax.broadcasted_iota(jnp.int32, (L, L), 0))
    col = _time_index(lax.broadcasted_iota(jnp.int32, (L, L), 1))
    causal = col <= row
    tril = causal.astype(BF16)
    lf = _log_sigmoid(g_col)
    lf_hi = lf.astype(BF16)
    r1 = lf - lf_hi.astype(F32)
    lf_mid = r1.astype(BF16)
    lf_lo = (r1 - lf_mid.astype(F32)).astype(BF16)
    b_all = _dot(tril, lf_hi) + _dot(tril, lf_mid) + _dot(tril, lf_lo)
    lane = lax.broadcasted_iota(jnp.int32, (L, LANES), 1)
    packed_row = jnp.where(lane < H, g_col, b_all).T
    return causal, b_all, packed_row


def _mlstm_head(h, q_f, k_f, v_f, g_col, causal, b_all, packed_row, c_st, n_st, m_st):
    L, dh = q_f.shape
    H = MLSTM_HEADS
    b_c = b_all[:, H + h:H + h + 1]
    i_c = g_col[:, h:h + 1]
    i_r = packed_row[h:h + 1, :]
    b_r = packed_row[H + h:H + h + 1, :]
    r_r = i_r - b_r
    m_prev = m_st[h][0:1, 0:1]

    dmat = jnp.where(causal, b_c + r_r, -jnp.inf)
    a_c = b_c + m_prev
    m_c = jnp.maximum(a_c, jnp.max(dmat, axis=-1, keepdims=True))
    w_inter = jnp.exp(a_c - m_c)
    pmat = jnp.exp(dmat - m_c)

    q_h = q_f.astype(BF16)
    k_h = k_f.astype(BF16)
    v_h = v_f.astype(BF16)
    qk = lax.dot_general(q_h, k_h, (((1,), (1,)), ((), ())), preferred_element_type=F32)
    s = qk * pmat
    c_prev = c_st[h]
    n_prev = n_st[h][0:1, :]
    num = _dot(s.astype(BF16), v_h) + w_inter * _dot(q_h, c_prev.astype(BF16))
    den = (jnp.sum(s, axis=-1, keepdims=True)
           + w_inter * jnp.sum(q_f * n_prev, axis=-1, keepdims=True))
    hh = num / jnp.maximum(jnp.abs(den), jnp.exp(-m_c))

    mu = jnp.mean(hh, axis=-1, keepdims=True)
    cen = hh - mu
    var = jnp.mean(cen * cen, axis=-1, keepdims=True)
    out = cen * lax.rsqrt(var + LN_EPS)

    b_end = b_c[L - 1:L, :]
    m_new = jnp.maximum(b_end + m_prev, jnp.max(b_end + r_r, axis=-1, keepdims=True))
    decay = jnp.exp(b_end + m_prev - m_new)
    wk = jnp.exp(b_end - b_c + i_c - m_new) * k_f
    kv = lax.dot_general(wk.astype(BF16), v_h, (((0,), (0,)), ((), ())),
                         preferred_element_type=F32)
    c_st[h] = decay * c_prev + kv
    n_st[h] = jnp.broadcast_to(decay * n_prev + jnp.sum(wk, axis=0, keepdims=True),
                               (SUBLANES, dh))
    m_st[h] = jnp.broadcast_to(m_new, (SUBLANES, LANES))
    return out


def _mixer_kernel(x_ref, g_ref, win_ref, bin_ref, mcw_ref, mcb_ref, bdq_ref, bdk_ref, bdv_ref,
                  wg_ref, bg_ref, ng_ref, sk_ref, wd_ref, dww_ref, dwb_ref, lng_ref, lnb_ref,
                  wpw_ref, bpw_ref, wout_ref, o_ref,
                  xm_buf, xm_hist, u_buf, u_hist, c_st, n_st, m_st, *, mconv, cconv):
    T, D = x_ref.shape
    W = wd_ref.shape[0]
    CW = wpw_ref.shape[0]
    H = MLSTM_HEADS
    dh = W // H
    assert T == TIME_BLOCK == MLSTM_CHUNK and dh == MXU_DIM and D == W == CW
    o1, o2, o3, o4, o5 = W, 2 * W, 2 * W + CW, 2 * W + 2 * CW, 2 * W + 2 * CW + D

    @pl.when(pl.program_id(1) == 0)
    def _():
        xm_hist[...] = jnp.zeros_like(xm_hist)
        u_hist[...] = jnp.zeros_like(u_hist)
        c_st[...] = jnp.zeros_like(c_st)
        n_st[...] = jnp.zeros_like(n_st)
        m_st[...] = jnp.zeros_like(m_st)

    x = x_ref[...]
    hb = _rmsnorm(x, g_ref[...]).astype(BF16)

    def proj(c0, c1):
        return _dot(hb, win_ref[:, c0:c1]) + bin_ref[:, c0:c1]

    xm = proj(0, o1)
    xc = _silu(_perm_causal_conv(xm, mcw_ref, mcb_ref, xm_buf, xm_hist, 0, mconv))
    _conv_stage(proj(o2, o3) * _sigmoid(proj(o3, o4)), u_buf, u_hist, 0, cconv)

    xc_b = xc.astype(BF16)
    xm_b = xm.astype(BF16)
    hs = [slice(h * dh, (h + 1) * dh) for h in range(H)]
    q = [_dot(xc_b[:, hs[h]], bdq_ref[h]) for h in range(H)]
    k = [_dot(xc_b[:, hs[h]], bdk_ref[h]) for h in range(H)]
    v = [_dot(xm_b[:, hs[h]], bdv_ref[h]) for h in range(H)]
    g_col = bg_ref[...]
    for n, part in enumerate((q, k, v)):
        for h in range(H):
            g_col = g_col + _dot(part[h].astype(BF16), wg_ref[n * W + h * dh:n * W + (h + 1) * dh, :])
    causal, b_all, packed_row = _mlstm_gates(g_col)

    lane_groups = list(range(0, CW, LANES))
    per_head = -(-len(lane_groups) // H)
    hn, conv, z, pgm, pgc = [], [], [], [], []
    for h in range(H):
        hn.append(_mlstm_head(h, q[h], k[h] * (dh ** -0.5), v[h], g_col, causal, b_all,
                              packed_row, c_st, n_st, m_st))
        for c in lane_groups[h * per_head:(h + 1) * per_head]:
            conv.append(_conv_lanes(u_buf, dww_ref, dwb_ref, c, cconv))
        z.append(proj(o1 + h * dh, o1 + (h + 1) * dh))
        pgm.append(proj(o4 + h * dh, o4 + (h + 1) * dh))
        pgc.append(proj(o5 + h * dh, o5 + (h + 1) * dh))
    hn = _cat(hn, axis=-1)
    acc = _cat(conv, axis=-1)

    ml_out = (hn * ng_ref[...] + sk_ref[...] * xc) * _silu(_cat(z, axis=-1))
    y_m = _dot(ml_out.astype(BF16), wd_ref[...])

    mu = jnp.mean(acc, axis=-1, keepdims=True)
    cen = acc - mu
    var = jnp.mean(cen * cen, axis=-1, keepdims=True)
    cv = _silu(cen * lax.rsqrt(var + LN_EPS) * lng_ref[...] + lnb_ref[...])
    y_c = _dot(cv.astype(BF16), wpw_ref[...]) + bpw_ref[...]

    merged = _sigmoid(_cat(pgm, axis=-1)) * y_m + _sigmoid(_cat(pgc, axis=-1)) * y_c
    o_ref[...] = x + _dot(merged.astype(BF16), wout_ref[...])


def _blockdiag_tiles(w):
    nb, bs, _ = w.shape
    col = jnp.arange(MXU_DIM)
    expand = (col[None, :] % bs == jnp.arange(bs)[:, None]).astype(w.dtype)
    spread = jnp.dot(w.reshape(nb * bs, bs), expand, precision=lax.Precision.HIGHEST)
    spread = spread.reshape(nb * bs // MXU_DIM, MXU_DIM, MXU_DIM)
    on_block = col[:, None] // bs == col[None, :] // bs
    return jnp.where(on_block[None], spread, 0.0)


def _row(v):
    return v.reshape(1, -1).astype(F32)


def _pad_rows(w):
    pad = -w.shape[0] % SUBLANES
    return jnp.pad(w.astype(F32), ((0, pad), (0, 0)))


def _mixer(x, p, tile):
    B, S, D = x.shape
    W = p["ml_w_down"].shape[0]
    CW = p["cv_w_pw"].shape[0]
    H = MLSTM_HEADS
    dh = W // H
    mconv = p["ml_conv_w"].shape[0]
    cconv = p["cv_dw_w"].shape[0]
    T = tile
    assert S % T == 0

    wg = jnp.zeros((3 * W, LANES), F32).at[:, :2 * H].set(p["ml_w_gate"]).astype(BF16)
    bg = jnp.zeros((1, LANES), F32).at[0, :2 * H].set(p["ml_b_gate"])

    args = [
        x, _row(p["norm_mix"]), p["w_in"], _row(p["b_in"]),
        _pad_rows(p["ml_conv_w"]), _row(p["ml_conv_b"]),
        _blockdiag_tiles(p["ml_wq"]).astype(BF16), _blockdiag_tiles(p["ml_wk"]).astype(BF16),
        _blockdiag_tiles(p["ml_wv"]).astype(BF16),
        wg, bg, _row(p["ml_norm_g"]), _row(p["ml_skip"]), p["ml_w_down"],
        _pad_rows(p["cv_dw_w"]), _row(p["cv_dw_b"]), _row(p["cv_ln_g"]), _row(p["cv_ln_b"]),
        p["cv_w_pw"], _row(p["cv_b_pw"]), p["w_out"],
    ]
    x_spec = pl.BlockSpec((None, T, D), lambda b, s: (b, s, 0))
    in_specs = [x_spec] + [_weight_spec(a) for a in args[1:]]
    return pl.pallas_call(
        functools.partial(_mixer_kernel, mconv=mconv, cconv=cconv),
        grid=(B, S // T),
        in_specs=in_specs,
        out_specs=x_spec,
        out_shape=jax.ShapeDtypeStruct((B, S, D), F32),
        scratch_shapes=[
            pltpu.VMEM((SUBLANES * (mconv - 1) + T, W), F32),
            pltpu.VMEM((SUBLANES * (mconv - 1), W), F32),
            pltpu.VMEM((SUBLANES * (cconv - 1) + T, CW), F32),
            pltpu.VMEM((SUBLANES * (cconv - 1), CW), F32),
            pltpu.VMEM((H, dh, dh), F32),
            pltpu.VMEM((H, SUBLANES, dh), F32),
            pltpu.VMEM((H, SUBLANES, LANES), F32),
        ],
        compiler_params=pltpu.CompilerParams(
            dimension_semantics=("arbitrary", "arbitrary"),
            vmem_limit_bytes=VMEM_LIMIT_BYTES),
        name="mixer",
    )(*[_operand(a) for a in args])


def _memkv_kernel(mem_ref, g_ref, wk_ref, wv_ref, kt_ref, v_ref):
    mn = _rmsnorm(mem_ref[...], g_ref[...]).astype(BF16)
    kt_ref[...] = _dot(mn, wk_ref[...]).T.astype(BF16)
    v_ref[...] = _dot(mn, wv_ref[...]).astype(BF16)


def _memkv(mem, g, wk, wv):
    B, N, D = mem.shape
    return pl.pallas_call(
        _memkv_kernel,
        grid=(B,),
        in_specs=[pl.BlockSpec((None, N, D), lambda b: (b, 0, 0)), _const_spec((1, D)),
                  _weight_spec(wk), _weight_spec(wv)],
        out_specs=[pl.BlockSpec((None, D, N), lambda b: (b, 0, 0)),
                   pl.BlockSpec((None, N, D), lambda b: (b, 0, 0))],
        out_shape=[jax.ShapeDtypeStruct((B, D, N), BF16), jax.ShapeDtypeStruct((B, N, D), BF16)],
        compiler_params=pltpu.CompilerParams(
            dimension_semantics=("arbitrary",), vmem_limit_bytes=VMEM_LIMIT_BYTES),
        name="memkv",
    )(mem, _row(g), _operand(wk), _operand(wv))


def _xattn_kernel(x_ref, g_ref, wq_ref, kt_ref, v_ref, wo_ref, o_ref):
    T, D = x_ref.shape
    dh = D // XATTN_HEADS
    x = x_ref[...]
    hb = _rmsnorm(x, g_ref[...]).astype(BF16)
    q = (_dot(hb, wq_ref[...]) * (dh ** -0.5)).astype(BF16)
    outs = []
    for h in range(XATTN_HEADS):
        sl = slice(h * dh, (h + 1) * dh)
        s = _dot(q[:, sl], kt_ref[sl, :])
        e = jnp.exp(s - jnp.max(s, axis=-1, keepdims=True))
        pr = e / jnp.sum(e, axis=-1, keepdims=True)
        outs.append(_dot(pr.astype(BF16), v_ref[:, sl]))
    o = jnp.concatenate(outs, axis=-1).astype(BF16)
    o_ref[...] = x + _dot(o, wo_ref[...])


def _xattn(x, kt, v, g, wq, wo, tile):
    B, S, D = x.shape
    N = v.shape[1]
    T = tile
    assert S % T == 0
    x_spec = pl.BlockSpec((None, T, D), lambda b, s: (b, s, 0))
    return pl.pallas_call(
        _xattn_kernel,
        grid=(B, S // T),
        in_specs=[x_spec, _const_spec((1, D)), _weight_spec(wq),
                  pl.BlockSpec((None, D, N), lambda b, s: (b, 0, 0)),
                  pl.BlockSpec((None, N, D), lambda b, s: (b, 0, 0)),
                  _weight_spec(wo)],
        out_specs=x_spec,
        out_shape=jax.ShapeDtypeStruct((B, S, D), F32),
        compiler_params=pltpu.CompilerParams(
            dimension_semantics=("arbitrary", "arbitrary"), vmem_limit_bytes=VMEM_LIMIT_BYTES),
        name="xattn",
    )(x, _row(g), _operand(wq), kt, v, _operand(wo))


def _ffn_kernel(x_ref, g_ref, wup_ref, bup_ref, dww_ref, dwb_ref, wdn_ref, fg_ref, o_ref,
                u_buf, u_hist, *, kconv, col_chunk, final_norm):
    T, D = x_ref.shape
    F = wdn_ref.shape[0]
    P = TIME_BLOCK

    @pl.when(pl.program_id(1) == 0)
    def _():
        u_hist[...] = jnp.zeros_like(u_hist)

    x = x_ref[...]
    hb = _rmsnorm(x, g_ref[...]).astype(BF16)

    def conv_cols(c0, c1):
        u = _dot(hb, wup_ref[:, c0:c1]) + bup_ref[:, c0:c1]
        return _cat([_perm_causal_conv(u[r:r + P], dww_ref, dwb_ref, u_buf, u_hist, c0, kconv)
                     for r in range(0, T, P)], axis=0)

    y = x
    for c0 in range(0, F, col_chunk):
        c1 = min(c0 + col_chunk, F)
        act = _silu(conv_cols(c0, c1)) * conv_cols(F + c0, F + c1)
        y = y + _dot(act.astype(BF16), wdn_ref[c0:c1, :])
    if final_norm:
        y = _rmsnorm(y, fg_ref[...])
    o_ref[...] = y


def _ffn(x, p, final_g, tile, final_norm):
    B, S, D = x.shape
    F = p["ffn_w_down"].shape[0]
    kconv = p["ffn_dw_w"].shape[0]
    T = tile
    col_chunk = -(-F // (2 * MXU_DIM)) * MXU_DIM
    assert S % T == 0 and T % TIME_BLOCK == 0 and F % LANES == 0
    args = [x, _row(p["norm_ffn"]), p["ffn_w_up"], _row(p["ffn_b_up"]),
            _pad_rows(p["ffn_dw_w"]),
            _row(p["ffn_dw_b"]), p["ffn_w_down"], _row(final_g)]
    x_spec = pl.BlockSpec((None, T, D), lambda b, s: (b, s, 0))
    return pl.pallas_call(
        functools.partial(_ffn_kernel, kconv=kconv, col_chunk=col_chunk, final_norm=final_norm),
        grid=(B, S // T),
        in_specs=[x_spec] + [_weight_spec(a) for a in args[1:]],
        out_specs=x_spec,
        out_shape=jax.ShapeDtypeStruct((B, S, D), F32),
        scratch_shapes=[pltpu.VMEM((SUBLANES * (kconv - 1) + TIME_BLOCK, 2 * F), F32),
                        pltpu.VMEM((SUBLANES * (kconv - 1), 2 * F), F32)],
        compiler_params=pltpu.CompilerParams(
            dimension_semantics=("arbitrary", "arbitrary"), vmem_limit_bytes=VMEM_LIMIT_BYTES),
        name="convffn",
    )(*[_operand(a) for a in args])


_LAYER_PARAMS = (
    "norm_mix", "w_in", "b_in", "ml_conv_w", "ml_conv_b", "ml_wq", "ml_wk", "ml_wv", "ml_w_gate",
    "ml_b_gate", "ml_norm_g", "ml_skip", "ml_w_down", "cv_dw_w", "cv_dw_b", "cv_ln_g", "cv_ln_b",
    "cv_w_pw", "cv_b_pw", "w_out", "norm_x", "norm_mem", "xa_wq", "xa_wk", "xa_wv", "xa_wo",
    "norm_ffn", "ffn_w_up", "ffn_b_up", "ffn_dw_w", "ffn_dw_b", "ffn_w_down")
_MATMUL_WEIGHTS = ("w_in", "ml_w_down", "cv_w_pw", "w_out", "xa_wq", "xa_wk", "xa_wv", "xa_wo",
                   "ffn_w_up", "ffn_w_down")


def kernel(x, mem, norm_mix, w_in, b_in, ml_conv_w, ml_conv_b, ml_wq, ml_wk, ml_wv, ml_w_gate, ml_b_gate, ml_norm_g, ml_skip, ml_w_down, cv_dw_w, cv_dw_b, cv_ln_g, cv_ln_b, cv_w_pw, cv_b_pw, w_out, norm_x, norm_mem, xa_wq, xa_wk, xa_wv, xa_wo, norm_ffn, ffn_w_up, ffn_b_up, ffn_dw_w, ffn_dw_b, ffn_w_down, final_norm):
    stacked = dict(zip(_LAYER_PARAMS, (
        norm_mix, w_in, b_in, ml_conv_w, ml_conv_b, ml_wq, ml_wk, ml_wv, ml_w_gate, ml_b_gate,
        ml_norm_g, ml_skip, ml_w_down, cv_dw_w, cv_dw_b, cv_ln_g, cv_ln_b, cv_w_pw, cv_b_pw,
        w_out, norm_x, norm_mem, xa_wq, xa_wk, xa_wv, xa_wo, norm_ffn, ffn_w_up, ffn_b_up,
        ffn_dw_w, ffn_dw_b, ffn_w_down)))
    depth = w_in.shape[0]
    B, S, D = x.shape
    G = TIME_BLOCK // SUBLANES
    assert S % TIME_BLOCK == 0
    x = x.reshape(B, S // TIME_BLOCK, SUBLANES, G, D).swapaxes(2, 3).reshape(B, S, D)
    cast = {name: stacked[name].astype(BF16) for name in _MATMUL_WEIGHTS}
    for l in range(depth):
        p = {name: (_LayerSlice(cast[name], l) if name in cast else val[l])
             for name, val in stacked.items()}
        x = _mixer(x, p, SEQ_TILE_MIXER)
        kt, v = _memkv(mem, p["norm_mem"], p["xa_wk"], p["xa_wv"])
        x = _xattn(x, kt, v, p["norm_x"], p["xa_wq"], p["xa_wo"], SEQ_TILE_XATTN)
        x = _ffn(x, p, final_norm, SEQ_TILE_FFN, final_norm=(l == depth - 1))
    return x.reshape(B, S // TIME_BLOCK, G, SUBLANES, D).swapaxes(2, 3).reshape(B, S, D)
```

```python
import functools

import jax
import jax.numpy as jnp
from jax import lax
from jax.experimental import pallas as pl
from jax.experimental.pallas import tpu as pltpu

F32 = jnp.float32
BF16 = jnp.bfloat16

MLSTM_HEADS = 4
XATTN_HEADS = 4
RMS_EPS = 1e-6
LN_EPS = 1e-5

LANES = 128
SUBLANES = 8
MXU_DIM = 256
MLSTM_CHUNK = 256
TIME_BLOCK = 256
SEQ_TILE_MIXER = TIME_BLOCK
SEQ_TILE_XATTN = 1024
SEQ_TILE_FFN = 512
VMEM_LIMIT_BYTES = 56 * 1024 * 1024


def _const_spec(shape):
    nd = len(shape)
    return pl.BlockSpec(shape, lambda *_: (0,) * nd, pipeline_mode=pl.Buffered(1))


class _LayerSlice:
    def __init__(self, stacked, layer):
        self.stacked = stacked
        self.layer = layer

    @property
    def shape(self):
        return tuple(self.stacked.shape[1:])


def _operand(a):
    return a.stacked if isinstance(a, _LayerSlice) else a


def _weight_spec(a):
    if not isinstance(a, _LayerSlice):
        return _const_spec(a.shape)
    index = (a.layer,) + (0,) * len(a.shape)
    return pl.BlockSpec((None,) + a.shape, lambda *_: index, pipeline_mode=pl.Buffered(1))


def _rmsnorm(x, g):
    return x * lax.rsqrt(jnp.mean(x * x, axis=-1, keepdims=True) + RMS_EPS) * g


def _sigmoid(x):
    return 1.0 / (1.0 + jnp.exp(-x))


def _silu(x):
    return x * _sigmoid(x)


def _log_sigmoid(x):
    return jnp.minimum(x, 0.0) - jnp.log(1.0 + jnp.exp(-jnp.abs(x)))


def _dot(a, b):
    return jnp.dot(a, b, preferred_element_type=F32)


def _cat(parts, axis):
    return parts[0] if len(parts) == 1 else jnp.concatenate(parts, axis=axis)


def _time_index(idx):
    return (idx % SUBLANES) * (TIME_BLOCK // SUBLANES) + idx // SUBLANES


def _conv_stage(u, ebuf, hist, col0, taps):
    P, cc = u.shape
    H = taps - 1
    assert P == TIME_BLOCK and SUBLANES * H <= P and cc % LANES == 0
    cs = slice(col0, col0 + cc)
    tail = u[P - SUBLANES * H:, :]
    rot = pltpu.roll(tail.reshape(H, SUBLANES, cc), 1, axis=1).reshape(SUBLANES * H, cc)
    first = lax.broadcasted_iota(jnp.int32, (SUBLANES * H, cc), 0) % SUBLANES == 0
    ebuf[0:SUBLANES * H, cs] = jnp.where(first, hist[:, cs], rot)
    hist[:, cs] = rot
    ebuf[SUBLANES * H:SUBLANES * H + P, cs] = u


def _conv_lanes(ebuf, w_ref, b_ref, c, taps):
    H = taps - 1
    P = TIME_BLOCK
    sl = slice(c, c + LANES)
    acc = ebuf[SUBLANES * H:SUBLANES * H + P, sl] * w_ref[H:H + 1, sl] + b_ref[:, sl]
    for lag in range(1, H + 1):
        r0 = SUBLANES * (H - lag)
        acc = acc + ebuf[r0:r0 + P, sl] * w_ref[H - lag:H - lag + 1, sl]
    return acc


def _perm_causal_conv(u, w_ref, b_ref, ebuf, hist, col0, taps):
    _conv_stage(u, ebuf, hist, col0, taps)
    return _cat([_conv_lanes(ebuf, w_ref, b_ref, c, taps)
                 for c in range(col0, col0 + u.shape[1], LANES)], axis=-1)


def _mlstm_gates(g_col):
    L = g_col.shape[0]
    H = MLSTM_HEADS
    row = _time_index(lax.broadcasted_iota(jnp.int32, (L, L), 0))
    col = _time_index(lax.broadcasted_iota(jnp.int32, (L, L), 1))
    causal = col <= row
    tril = causal.astype(BF16)
    lf = _log_sigmoid(g_col)
    lf_hi = lf.astype(BF16)
    r1 = lf - lf_hi.astype(F32)
    lf_mid = r1.astype(BF16)
    lf_lo = (r1 - lf_mid.astype(F32)).astype(BF16)
    b_all = _dot(tril, lf_hi) + _dot(tril, lf_mid) + _dot(tril, lf_lo)
    lane = lax.broadcasted_iota(jnp.int32, (L, LANES), 1)
    packed_row = jnp.where(lane < H, g_col, b_all).T
    return causal, b_all, packed_row


def _mlstm_head(h, q_f, k_f, v_f, g_col, causal, b_all, packed_row, c_st, n_st, m_st):
    L, dh = q_f.shape
    H = MLSTM_HEADS
    b_c = b_all[:, H + h:H + h + 1]
    i_c = g_col[:, h:h + 1]
    i_r = packed_row[h:h + 1, :]
    b_r = packed_row[H + h:H + h + 1, :]
    r_r = i_r - b_r
    m_prev = m_st[h][0:1, 0:1]

    dmat = jnp.where(causal, b_c + r_r, -jnp.inf)
    a_c = b_c + m_prev
    m_c = jnp.maximum(a_c, jnp.max(dmat, axis=-1, keepdims=True))
    w_inter = jnp.exp(a_c - m_c)
    pmat = jnp.exp(dmat - m_c)

    q_h = q_f.astype(BF16)
    k_h = k_f.astype(BF16)
    v_h = v_f.astype(BF16)
    qk = lax.dot_general(q_h, k_h, (((1,), (1,)), ((), ())), preferred_element_type=F32)
    s = qk * pmat
    c_prev = c_st[h]
    n_prev = n_st[h][0:1, :]
    num = _dot(s.astype(BF16), v_h) + w_inter * _dot(q_h, c_prev.astype(BF16))
    den = (jnp.sum(s, axis=-1, keepdims=True)
           + w_inter * jnp.sum(q_f * n_prev, axis=-1, keepdims=True))
    hh = num / jnp.maximum(jnp.abs(den), jnp.exp(-m_c))

    mu = jnp.mean(hh, axis=-1, keepdims=True)
    cen = hh - mu
    var = jnp.mean(cen * cen, axis=-1, keepdims=True)
    out = cen * lax.rsqrt(var + LN_EPS)

    b_end = b_c[L - 1:L, :]
    m_new = jnp.maximum(b_end + m_prev, jnp.max(b_end + r_r, axis=-1, keepdims=True))
    decay = jnp.exp(b_end + m_prev - m_new)
    wk = jnp.exp(b_end - b_c + i_c - m_new) * k_f
    kv = lax.dot_general(wk.astype(BF16), v_h, (((0,), (0,)), ((), ())),
                         preferred_element_type=F32)
    c_st[h] = decay * c_prev + kv
    n_st[h] = jnp.broadcast_to(decay * n_prev + jnp.sum(wk, axis=0, keepdims=True),
                               (SUBLANES, dh))
    m_st[h] = jnp.broadcast_to(m_new, (SUBLANES, LANES))
    return out


def _mixer_kernel(x_ref, g_ref, win_ref, bin_ref, mcw_ref, mcb_ref, bdq_ref, bdk_ref, bdv_ref,
                  wg_ref, bg_ref, ng_ref, sk_ref, wd_ref, dww_ref, dwb_ref, lng_ref, lnb_ref,
                  wpw_ref, bpw_ref, wout_ref, o_ref,
                  xm_buf, xm_hist, u_buf, u_hist, c_st, n_st, m_st, *, mconv, cconv):
    T, D = x_ref.shape
    W = wd_ref.shape[0]
    CW = wpw_ref.shape[0]
    H = MLSTM_HEADS
    dh = W // H
    assert T == TIME_BLOCK == MLSTM_CHUNK and dh == MXU_DIM and D == W == CW
    o1, o2, o3, o4, o5 = W, 2 * W, 2 * W + CW, 2 * W + 2 * CW, 2 * W + 2 * CW + D

    @pl.when(pl.program_id(1) == 0)
    def _():
        xm_hist[...] = jnp.zeros_like(xm_hist)
        u_hist[...] = jnp.zeros_like(u_hist)
        c_st[...] = jnp.zeros_like(c_st)
        n_st[...] = jnp.zeros_like(n_st)
        m_st[...] = jnp.zeros_like(m_st)

    x = x_ref[...]
    hb = _rmsnorm(x, g_ref[...]).astype(BF16)

    def proj(c0, c1):
        return _dot(hb, win_ref[:, c0:c1]) + bin_ref[:, c0:c1]

    xm = proj(0, o1)
    xc = _silu(_perm_causal_conv(xm, mcw_ref, mcb_ref, xm_buf, xm_hist, 0, mconv))
    _conv_stage(proj(o2, o3) * _sigmoid(proj(o3, o4)), u_buf, u_hist, 0, cconv)

    xc_b = xc.astype(BF16)
    xm_b = xm.astype(BF16)
    hs = [slice(h * dh, (h + 1) * dh) for h in range(H)]
    q = [_dot(xc_b[:, hs[h]], bdq_ref[h]) for h in range(H)]
    k = [_dot(xc_b[:, hs[h]], bdk_ref[h]) for h in range(H)]
    v = [_dot(xm_b[:, hs[h]], bdv_ref[h]) for h in range(H)]
    g_col = bg_ref[...]
    for n, part in enumerate((q, k, v)):
        for h in range(H):
            g_col = g_col + _dot(part[h].astype(BF16), wg_ref[n * W + h * dh:n * W + (h + 1) * dh, :])
    causal, b_all, packed_row = _mlstm_gates(g_col)

    lane_groups = list(range(0, CW, LANES))
    per_head = -(-len(lane_groups) // H)
    hn, conv, z, pgm, pgc = [], [], [], [], []
    for h in range(H):
        hn.append(_mlstm_head(h, q[h], k[h] * (dh ** -0.5), v[h], g_col, causal, b_all,
                              packed_row, c_st, n_st, m_st))
        for c in lane_groups[h * per_head:(h + 1) * per_head]:
            conv.append(_conv_lanes(u_buf, dww_ref, dwb_ref, c, cconv))
        z.append(proj(o1 + h * dh, o1 + (h + 1) * dh))
        pgm.append(proj(o4 + h * dh, o4 + (h + 1) * dh))
        pgc.append(proj(o5 + h * dh, o5 + (h + 1) * dh))
    hn = _cat(hn, axis=-1)
    acc = _cat(conv, axis=-1)

    ml_out = (hn * ng_ref[...] + sk_ref[...] * xc) * _silu(_cat(z, axis=-1))
    y_m = _dot(ml_out.astype(BF16), wd_ref[...])

    mu = jnp.mean(acc, axis=-1, keepdims=True)
    cen = acc - mu
    var = jnp.mean(cen * cen, axis=-1, keepdims=True)
    cv = _silu(cen * lax.rsqrt(var + LN_EPS) * lng_ref[...] + lnb_ref[...])
    y_c = _dot(cv.astype(BF16), wpw_ref[...]) + bpw_ref[...]

    merged = _sigmoid(_cat(pgm, axis=-1)) * y_m + _sigmoid(_cat(pgc, axis=-1)) * y_c
    o_ref[...] = x + _dot(merged.astype(BF16), wout_ref[...])


def _blockdiag_tiles(w):
    nb, bs, _ = w.shape
    col = jnp.arange(MXU_DIM)
    expand = (col[None, :] % bs == jnp.arange(bs)[:, None]).astype(w.dtype)
    spread = jnp.dot(w.reshape(nb * bs, bs), expand, precision=lax.Precision.HIGHEST)
    spread = spread.reshape(nb * bs // MXU_DIM, MXU_DIM, MXU_DIM)
    on_block = col[:, None] // bs == col[None, :] // bs
    return jnp.where(on_block[None], spread, 0.0)


def _row(v):
    return v.reshape(1, -1).astype(F32)


def _pad_rows(w):
    pad = -w.shape[0] % SUBLANES
    return jnp.pad(w.astype(F32), ((0, pad), (0, 0)))


def _mixer(x, p, tile):
    B, S, D = x.shape
    W = p["ml_w_down"].shape[0]
    CW = p["cv_w_pw"].shape[0]
    H = MLSTM_HEADS
    dh = W // H
    mconv = p["ml_conv_w"].shape[0]
    cconv = p["cv_dw_w"].shape[0]
    T = tile
    assert S % T == 0

    wg = jnp.zeros((3 * W, LANES), F32).at[:, :2 * H].set(p["ml_w_gate"]).astype(BF16)
    bg = jnp.zeros((1, LANES), F32).at[0, :2 * H].set(p["ml_b_gate"])

    args = [
        x, _row(p["norm_mix"]), p["w_in"], _row(p["b_in"]),
        _pad_rows(p["ml_conv_w"]), _row(p["ml_conv_b"]),
        _blockdiag_tiles(p["ml_wq"]).astype(BF16), _blockdiag_tiles(p["ml_wk"]).astype(BF16),
        _blockdiag_tiles(p["ml_wv"]).astype(BF16),
        wg, bg, _row(p["ml_norm_g"]), _row(p["ml_skip"]), p["ml_w_down"],
        _pad_rows(p["cv_dw_w"]), _row(p["cv_dw_b"]), _row(p["cv_ln_g"]), _row(p["cv_ln_b"]),
        p["cv_w_pw"], _row(p["cv_b_pw"]), p["w_out"],
    ]
    x_spec = pl.BlockSpec((None, T, D), lambda b, s: (b, s, 0))
    in_specs = [x_spec] + [_weight_spec(a) for a in args[1:]]
    return pl.pallas_call(
        functools.partial(_mixer_kernel, mconv=mconv, cconv=cconv),
        grid=(B, S // T),
        in_specs=in_specs,
        out_specs=x_spec,
        out_shape=jax.ShapeDtypeStruct((B, S, D), F32),
        scratch_shapes=[
            pltpu.VMEM((SUBLANES * (mconv - 1) + T, W), F32),
            pltpu.VMEM((SUBLANES * (mconv - 1), W), F32),
            pltpu.VMEM((SUBLANES * (cconv - 1) + T, CW), F32),
            pltpu.VMEM((SUBLANES * (cconv - 1), CW), F32),
            pltpu.VMEM((H, dh, dh), F32),
            pltpu.VMEM((H, SUBLANES, dh), F32),
            pltpu.VMEM((H, SUBLANES, LANES), F32),
        ],
        compiler_params=pltpu.CompilerParams(
            dimension_semantics=("arbitrary", "arbitrary"),
            vmem_limit_bytes=VMEM_LIMIT_BYTES),
        name="mixer",
    )(*[_operand(a) for a in args])


def _memkv_kernel(mem_ref, g_ref, wk_ref, wv_ref, kt_ref, v_ref):
    mn = _rmsnorm(mem_ref[...], g_ref[...]).astype(BF16)
    kt_ref[...] = _dot(mn, wk_ref[...]).T.astype(BF16)
    v_ref[...] = _dot(mn, wv_ref[...]).astype(BF16)


def _memkv(mem, g, wk, wv):
    B, N, D = mem.shape
    return pl.pallas_call(
        _memkv_kernel,
        grid=(B,),
        in_specs=[pl.BlockSpec((None, N, D), lambda b: (b, 0, 0)), _const_spec((1, D)),
                  _weight_spec(wk), _weight_spec(wv)],
        out_specs=[pl.BlockSpec((None, D, N), lambda b: (b, 0, 0)),
                   pl.BlockSpec((None, N, D), lambda b: (b, 0, 0))],
        out_shape=[jax.ShapeDtypeStruct((B, D, N), BF16), jax.ShapeDtypeStruct((B, N, D), BF16)],
        compiler_params=pltpu.CompilerParams(
            dimension_semantics=("arbitrary",), vmem_limit_bytes=VMEM_LIMIT_BYTES),
        name="memkv",
    )(mem, _row(g), _operand(wk), _operand(wv))


def _xattn_kernel(x_ref, g_ref, wq_ref, kt_ref, v_ref, wo_ref, o_ref):
    T, D = x_ref.shape
    dh = D // XATTN_HEADS
    x = x_ref[...]
    hb = _rmsnorm(x, g_ref[...]).astype(BF16)
    q = (_dot(hb, wq_ref[...]) * (dh ** -0.5)).astype(BF16)
    outs = []
    for h in range(XATTN_HEADS):
        sl = slice(h * dh, (h + 1) * dh)
        s = _dot(q[:, sl], kt_ref[sl, :])
        e = jnp.exp(s - jnp.max(s, axis=-1, keepdims=True))
        pr = e / jnp.sum(e, axis=-1, keepdims=True)
        outs.append(_dot(pr.astype(BF16), v_ref[:, sl]))
    o = jnp.concatenate(outs, axis=-1).astype(BF16)
    o_ref[...] = x + _dot(o, wo_ref[...])


def _xattn(x, kt, v, g, wq, wo, tile):
    B, S, D = x.shape
    N = v.shape[1]
    T = tile
    assert S % T == 0
    x_spec = pl.BlockSpec((None, T, D), lambda b, s: (b, s, 0))
    return pl.pallas_call(
        _xattn_kernel,
        grid=(B, S // T),
        in_specs=[x_spec, _const_spec((1, D)), _weight_spec(wq),
                  pl.BlockSpec((None, D, N), lambda b, s: (b, 0, 0)),
                  pl.BlockSpec((None, N, D), lambda b, s: (b, 0, 0)),
                  _weight_spec(wo)],
        out_specs=x_spec,
        out_shape=jax.ShapeDtypeStruct((B, S, D), F32),
        compiler_params=pltpu.CompilerParams(
            dimension_semantics=("arbitrary", "arbitrary"), vmem_limit_bytes=VMEM_LIMIT_BYTES),
        name="xattn",
    )(x, _row(g), _operand(wq), kt, v, _operand(wo))


def _ffn_kernel(x_ref, g_ref, wup_ref, bup_ref, dww_ref, dwb_ref, wdn_ref, fg_ref, o_ref,
                u_buf, u_hist, *, kconv, col_chunk, final_norm):
    T, D = x_ref.shape
    F = wdn_ref.shape[0]
    P = TIME_BLOCK

    @pl.when(pl.program_id(1) == 0)
    def _():
        u_hist[...] = jnp.zeros_like(u_hist)

    x = x_ref[...]
    hb = _rmsnorm(x, g_ref[...]).astype(BF16)

    def conv_cols(c0, c1):
        u = _dot(hb, wup_ref[:, c0:c1]) + bup_ref[:, c0:c1]
        return _cat([_perm_causal_conv(u[r:r + P], dww_ref, dwb_ref, u_buf, u_hist, c0, kconv)
                     for r in range(0, T, P)], axis=0)

    y = x
    for c0 in range(0, F, col_chunk):
        c1 = min(c0 + col_chunk, F)
        act = _silu(conv_cols(c0, c1)) * conv_cols(F + c0, F + c1)
        y = y + _dot(act.astype(BF16), wdn_ref[c0:c1, :])
    if final_norm:
        y = _rmsnorm(y, fg_ref[...])
        y = y.reshape(T // P, P // SUBLANES, SUBLANES, D).swapaxes(1, 2).reshape(T, D)
    o_ref[...] = y


def _ffn(x, p, final_g, tile, final_norm):
    B, S, D = x.shape
    F = p["ffn_w_down"].shape[0]
    kconv = p["ffn_dw_w"].shape[0]
    T = tile
    col_chunk = -(-F // (2 * MXU_DIM)) * MXU_DIM
    assert S % T == 0 and T % TIME_BLOCK == 0 and F % LANES == 0
    args = [x, _row(p["norm_ffn"]), p["ffn_w_up"], _row(p["ffn_b_up"]),
            _pad_rows(p["ffn_dw_w"]),
            _row(p["ffn_dw_b"]), p["ffn_w_down"], _row(final_g)]
    x_spec = pl.BlockSpec((None, T, D), lambda b, s: (b, s, 0))
    return pl.pallas_call(
        functools.partial(_ffn_kernel, kconv=kconv, col_chunk=col_chunk, final_norm=final_norm),
        grid=(B, S // T),
        in_specs=[x_spec] + [_weight_spec(a) for a in args[1:]],
        out_specs=x_spec,
        out_shape=jax.ShapeDtypeStruct((B, S, D), F32),
        scratch_shapes=[pltpu.VMEM((SUBLANES * (kconv - 1) + TIME_BLOCK, 2 * F), F32),
                        pltpu.VMEM((SUBLANES * (kconv - 1), 2 * F), F32)],
        compiler_params=pltpu.CompilerParams(
            dimension_semantics=("arbitrary", "arbitrary"), vmem_limit_bytes=VMEM_LIMIT_BYTES),
        name="convffn",
    )(*[_operand(a) for a in args])


_LAYER_PARAMS = (
    "norm_mix", "w_in", "b_in", "ml_conv_w", "ml_conv_b", "ml_wq", "ml_wk", "ml_wv", "ml_w_gate",
    "ml_b_gate", "ml_norm_g", "ml_skip", "ml_w_down", "cv_dw_w", "cv_dw_b", "cv_ln_g", "cv_ln_b",
    "cv_w_pw", "cv_b_pw", "w_out", "norm_x", "norm_mem", "xa_wq", "xa_wk", "xa_wv", "xa_wo",
    "norm_ffn", "ffn_w_up", "ffn_b_up", "ffn_dw_w", "ffn_dw_b", "ffn_w_down")
_MATMUL_WEIGHTS = ("w_in", "ml_w_down", "cv_w_pw", "w_out", "xa_wq", "xa_wk", "xa_wv", "xa_wo",
                   "ffn_w_up", "ffn_w_down")


def kernel(x, mem, norm_mix, w_in, b_in, ml_conv_w, ml_conv_b, ml_wq, ml_wk, ml_wv, ml_w_gate, ml_b_gate, ml_norm_g, ml_skip, ml_w_down, cv_dw_w, cv_dw_b, cv_ln_g, cv_ln_b, cv_w_pw, cv_b_pw, w_out, norm_x, norm_mem, xa_wq, xa_wk, xa_wv, xa_wo, norm_ffn, ffn_w_up, ffn_b_up, ffn_dw_w, ffn_dw_b, ffn_w_down, final_norm):
    stacked = dict(zip(_LAYER_PARAMS, (
        norm_mix, w_in, b_in, ml_conv_w, ml_conv_b, ml_wq, ml_wk, ml_wv, ml_w_gate, ml_b_gate,
        ml_norm_g, ml_skip, ml_w_down, cv_dw_w, cv_dw_b, cv_ln_g, cv_ln_b, cv_w_pw, cv_b_pw,
        w_out, norm_x, norm_mem, xa_wq, xa_wk, xa_wv, xa_wo, norm_ffn, ffn_w_up, ffn_b_up,
        ffn_dw_w, ffn_dw_b, ffn_w_down)))
    depth = w_in.shape[0]
    B, S, D = x.shape
    G = TIME_BLOCK // SUBLANES
    assert S % TIME_BLOCK == 0
    x = x.reshape(B, S // TIME_BLOCK, SUBLANES, G, D).swapaxes(2, 3).reshape(B, S, D)
    cast = {name: stacked[name].astype(BF16) for name in _MATMUL_WEIGHTS}
    for l in range(depth):
        p = {name: (_LayerSlice(cast[name], l) if name in cast else val[l])
             for name, val in stacked.items()}
        x = _mixer(x, p, SEQ_TILE_MIXER)
        kt, v = _memkv(mem, p["norm_mem"], p["xa_wk"], p["xa_wv"])
        x = _xattn(x, kt, v, p["norm_x"], p["xa_wq"], p["xa_wo"], SEQ_TILE_XATTN)
        x = _ffn(x, p, final_norm, SEQ_TILE_FFN, final_norm=(l == depth - 1))
    return x
```
